```python
import math
import jax, jax.numpy as jnp
from jax import lax
import numpy as np

D_MODEL = 1024
BATCH = 4
SEQ = 4096
DEPTH = 1

MIX_WIDTH = D_MODEL
POOL_WIDTH = D_MODEL // 2
POOL_WINDOWS = (2, 4, 8, 16)
POOL_GROUPS = len(POOL_WINDOWS)
POOL_GROUP_DIM = POOL_WIDTH // POOL_GROUPS

N_HEADS = 8
QK_NOPE_DIM = 64
QK_ROPE_DIM = 32
V_HEAD_DIM = 64
QK_HEAD_DIM = QK_NOPE_DIM + QK_ROPE_DIM
ATTN_WIDTH = N_HEADS * V_HEAD_DIM
Q_LORA_RANK = 384
KV_LORA_RANK = 256
ROPE_THETA = 10000.0
Q_BLOCK = 128

IN_WIDTH = POOL_WIDTH + Q_LORA_RANK + KV_LORA_RANK + QK_ROPE_DIM

FFN_HIDDEN = int(math.ceil(8 * D_MODEL / 3 / 256) * 256)

DEEPNORM_ALPHA = (2.0 * DEPTH) ** 0.25
DEEPNORM_BETA = (8.0 * DEPTH) ** -0.25
LN_EPS = 1e-5
RMS_EPS = 1e-6

kernel_name = "hybrid_pool_mla_deepnorm_encoder"


def layer_norm(x, g, b):
    xf = x.astype(jnp.float32)
    mu = jnp.mean(xf, axis=-1, keepdims=True)
    var = jnp.mean(jnp.square(xf - mu), axis=-1, keepdims=True)
    y = (xf - mu) * lax.rsqrt(var + LN_EPS) * g.astype(jnp.float32) + b.astype(jnp.float32)
    return y.astype(x.dtype)


def rms_norm(x, g):
    xf = x.astype(jnp.float32)
    y = xf * lax.rsqrt(jnp.mean(jnp.square(xf), axis=-1, keepdims=True) + RMS_EPS)
    return (y * g.astype(jnp.float32)).astype(x.dtype)


def rope_tables(positions):
    inv_freq = 1.0 / (ROPE_THETA ** (jnp.arange(0, QK_ROPE_DIM, 2, dtype=jnp.float32) / QK_ROPE_DIM))
    ang = positions.astype(jnp.float32)[..., None] * inv_freq
    return jnp.cos(ang)[:, :, None, :], jnp.sin(ang)[:, :, None, :]


def apply_rope(t, cos, sin):
    tf = t.astype(jnp.float32)
    t1, t2 = jnp.split(tf, 2, axis=-1)
    out = jnp.concatenate([t1 * cos - t2 * sin, t2 * cos + t1 * sin], axis=-1)
    return out.astype(t.dtype)


def centred_mean_minus_self(u, window):
    s = u.shape[1]
    uf = u.astype(jnp.float32)
    cs = jnp.concatenate([jnp.zeros_like(uf[:, :1]), jnp.cumsum(uf, axis=1)], axis=1)
    idx = jnp.arange(s)
    lo = jnp.clip(idx - window // 2, 0, s)
    hi = jnp.clip(idx + window - window // 2, 0, s)
    win_sum = jnp.take(cs, hi, axis=1) - jnp.take(cs, lo, axis=1)
    count = (hi - lo).astype(jnp.float32)[None, :, None]
    return (win_sum / count - uf).astype(u.dtype)


def pool_mixer(u, pool_w, pool_scale):
    outs = []
    for g, w in enumerate(POOL_WINDOWS):
        ug = u[..., g * POOL_GROUP_DIM:(g + 1) * POOL_GROUP_DIM]
        pg = centred_mean_minus_self(ug, w)
        outs.append(jnp.einsum('bsc,cd->bsd', pg, pool_w[g]))
    return jnp.concatenate(outs, axis=-1) * pool_scale


def mla_mixer(cq, ckv, kr, cos, sin, q_norm_g, w_q_up, kv_norm_g, w_k_up, w_v_up):
    b, s, _ = cq.shape
    cq = rms_norm(cq, q_norm_g)
    q = jnp.einsum('bsr,re->bse', cq, w_q_up).reshape(b, s, N_HEADS, QK_HEAD_DIM)
    q = jnp.concatenate([q[..., :QK_NOPE_DIM], apply_rope(q[..., QK_NOPE_DIM:], cos, sin)], axis=-1)
    ckv = rms_norm(ckv, kv_norm_g)
    k_nope = jnp.einsum('bsr,re->bse', ckv, w_k_up).reshape(b, s, N_HEADS, QK_NOPE_DIM)
    v = jnp.einsum('bsr,re->bse', ckv, w_v_up).reshape(b, s, N_HEADS, V_HEAD_DIM)
    k_rope = apply_rope(kr[:, :, None, :], cos, sin)
    k = jnp.concatenate([k_nope, jnp.broadcast_to(k_rope, (b, s, N_HEADS, QK_ROPE_DIM))], axis=-1)
    scale = QK_HEAD_DIM ** -0.5
    n_blocks = s // Q_BLOCK
    q_blocks = q.reshape(b, n_blocks, Q_BLOCK, N_HEADS, QK_HEAD_DIM).transpose(1, 0, 2, 3, 4)

    def attend(qb):
        scores = jnp.einsum('bqhd,bkhd->bhqk', qb, k).astype(jnp.float32) * scale
        p = jax.nn.softmax(scores, axis=-1).astype(v.dtype)
        return jnp.einsum('bhqk,bkhd->bqhd', p, v)

    out = lax.map(attend, q_blocks)
    return out.transpose(1, 0, 2, 3, 4).reshape(b, s, ATTN_WIDTH)


def setup_inputs(seed: int = 0) -> dict:
    key = jax.random.key(seed)
    ks = jax.random.split(key, 20)
    L = DEPTH

    def w(k, shape, fan_in, gain=1.0):
        return jax.random.normal(k, shape, jnp.float32) * (fan_in ** -0.5) * gain

    x = jax.random.normal(ks[0], (BATCH, SEQ, D_MODEL), jnp.float32)
    positions = jnp.broadcast_to(jnp.arange(SEQ, dtype=jnp.int32)[None, :], (BATCH, SEQ))
    return {
        "x": x,
        "positions": positions,
        "w_in": w(ks[1], (L, D_MODEL, IN_WIDTH), D_MODEL),
        "pool_w": w(ks[2], (L, POOL_GROUPS, POOL_GROUP_DIM, POOL_GROUP_DIM), POOL_GROUP_DIM, DEEPNORM_BETA),
        "pool_scale": 1.0 + 0.01 * jax.random.normal(ks[3], (L, POOL_WIDTH), jnp.float32),
        "q_norm_g": 1.0 + 0.01 * jax.random.normal(ks[4], (L, Q_LORA_RANK), jnp.float32),
        "w_q_up": w(ks[5], (L, Q_LORA_RANK, N_HEADS * QK_HEAD_DIM), Q_LORA_RANK),
        "kv_norm_g": 1.0 + 0.01 * jax.random.normal(ks[6], (L, KV_LORA_RANK), jnp.float32),
        "w_k_up": w(ks[7], (L, KV_LORA_RANK, N_HEADS * QK_NOPE_DIM), KV_LORA_RANK),
        "w_v_up": w(ks[8], (L, KV_LORA_RANK, N_HEADS * V_HEAD_DIM), KV_LORA_RANK, DEEPNORM_BETA),
        "w_o": w(ks[9], (L, MIX_WIDTH, D_MODEL), MIX_WIDTH, DEEPNORM_BETA),
        "ln1_g": 1.0 + 0.01 * jax.random.normal(ks[10], (L, D_MODEL), jnp.float32),
        "ln1_b": 0.01 * jax.random.normal(ks[11], (L, D_MODEL), jnp.float32),
        "w_gate": w(ks[12], (L, D_MODEL, FFN_HIDDEN), D_MODEL, DEEPNORM_BETA),
        "w_up": w(ks[13], (L, D_MODEL, FFN_HIDDEN), D_MODEL, DEEPNORM_BETA),
        "w_down": w(ks[14], (L, FFN_HIDDEN, D_MODEL), FFN_HIDDEN, DEEPNORM_BETA),
        "ln2_g": 1.0 + 0.01 * jax.random.normal(ks[15], (L, D_MODEL), jnp.float32),
        "ln2_b": 0.01 * jax.random.normal(ks[16], (L, D_MODEL), jnp.float32),
    }


def reference(x, positions, w_in, pool_w, pool_scale, q_norm_g, w_q_up, kv_norm_g, w_k_up, w_v_up,
              w_o, ln1_g, ln1_b, w_gate, w_up, w_down, ln2_g, ln2_b):
    cos, sin = rope_tables(positions)
    o_q = POOL_WIDTH
    o_kv = o_q + Q_LORA_RANK
    o_kr = o_kv + KV_LORA_RANK
    for l in range(DEPTH):
        h = jnp.einsum('bsd,de->bse', x, w_in[l])
        pool_out = pool_mixer(h[..., :o_q], pool_w[l], pool_scale[l])
        attn_out = mla_mixer(h[..., o_q:o_kv], h[..., o_kv:o_kr], h[..., o_kr:], cos, sin,
                             q_norm_g[l], w_q_up[l], kv_norm_g[l], w_k_up[l], w_v_up[l])
        mix = jnp.einsum('bse,ed->bsd', jnp.concatenate([pool_out, attn_out], axis=-1), w_o[l])
        x = layer_norm(DEEPNORM_ALPHA * x + mix, ln1_g[l], ln1_b[l])
        gate = jnp.einsum('bsd,df->bsf', x, w_gate[l])
        up = jnp.einsum('bsd,df->bsf', x, w_up[l])
        ffn = jnp.einsum('bsf,fd->bsd', jax.nn.silu(gate) * up, w_down[l])
        x = layer_norm(DEEPNORM_ALPHA * x + ffn, ln2_g[l], ln2_b[l])
    return x
```

```python
import functools
import math

import jax
import jax.numpy as jnp
from jax import lax
from jax.experimental import pallas as pl
from jax.experimental.pallas import tpu as pltpu

D_MODEL = 1024
POOL_WIDTH = 512
POOL_WINDOWS = (2, 4, 8, 16)
POOL_GROUP_DIM = 128
N_HEADS = 8
QK_NOPE_DIM = 64
QK_ROPE_DIM = 32
V_HEAD_DIM = 64
QK_HEAD_DIM = QK_NOPE_DIM + QK_ROPE_DIM
ATTN_WIDTH = N_HEADS * V_HEAD_DIM
Q_LORA_RANK = 384
KV_LORA_RANK = 256
ROPE_THETA = 10000.0
FFN_HIDDEN = 2816
LN_EPS = 1e-5
RMS_EPS = 1e-6

LANES = 128
HEAD_SLOT = LANES
ROPE_LO = QK_NOPE_DIM
ROPE_HALF = QK_ROPE_DIM // 2
H_WIDTH = POOL_WIDTH + Q_LORA_RANK + KV_LORA_RANK + LANES
POOL_HALO = 8
VMEM_LIMIT = 56 * 1024 * 1024

BF16 = jnp.bfloat16
F32 = jnp.float32


def _rope(t, cos, sin, lane):
    fwd = pltpu.roll(t, ROPE_HALF, 1)
    bwd = pltpu.roll(t, LANES - ROPE_HALF, 1)
    rot = jnp.where(lane < ROPE_LO + ROPE_HALF, -bwd, fwd)
    return t * cos + rot * sin


def _rms(t, g):
    y = t * lax.rsqrt(jnp.mean(t * t, axis=-1, keepdims=True) + RMS_EPS)
    return y * g


def _layer_norm(y, g, b):
    mu = jnp.mean(y, axis=-1, keepdims=True)
    d = y - mu
    var = jnp.mean(d * d, axis=-1, keepdims=True)
    return d * lax.rsqrt(var + LN_EPS) * g + b


def _proj_kernel(x_ref, pos_ref, invf_ref, w_in_ref, qg_ref, wq_ref, kvg_ref, wk_ref, wv_ref,
                 u_ref, q_ref, k_ref, v_ref, *, q_scale):
    x = x_ref[0].astype(BF16)
    h = jnp.dot(x, w_in_ref[...], preferred_element_type=F32)
    u_ref[0] = h[:, :POOL_WIDTH]
    o_kv = POOL_WIDTH + Q_LORA_RANK
    o_kr = o_kv + KV_LORA_RANK
    cq = _rms(h[:, POOL_WIDTH:o_kv], qg_ref[...])
    ckv = _rms(h[:, o_kv:o_kr], kvg_ref[...])
    kr = h[:, o_kr:]

    ang = pos_ref[0].astype(F32) * invf_ref[...]
    cos = jnp.cos(ang)
    sin = jnp.sin(ang)
    lane = lax.broadcasted_iota(jnp.int32, ang.shape, 1)

    q = jnp.dot(cq.astype(BF16), wq_ref[...], preferred_element_type=F32)
    ckv_b = ckv.astype(BF16)
    k_nope = jnp.dot(ckv_b, wk_ref[...], preferred_element_type=F32)
    v = jnp.dot(ckv_b, wv_ref[...], preferred_element_type=F32)
    v_ref[0] = v.astype(BF16)

    k_rope = _rope(kr, cos, sin, lane)
    for hd in range(N_HEADS):
        sl = slice(hd * HEAD_SLOT, (hd + 1) * HEAD_SLOT)
        q_ref[0, :, sl] = (_rope(q[:, sl], cos, sin, lane) * q_scale).astype(BF16)
        k_ref[0, :, sl] = (k_nope[:, sl] + k_rope).astype(BF16)


def _attn_kernel(q_ref, k_ref, v_ref, o_ref):
    v = v_ref[0]
    outs = []
    for j in range(2):
        sl = slice(j * HEAD_SLOT, (j + 1) * HEAD_SLOT)
        s = lax.dot_general(q_ref[0, :, sl], k_ref[0, :, sl], (((1,), (1,)), ((), ())),
                            preferred_element_type=F32)
        m = jnp.max(s, axis=-1, keepdims=True)
        p = jnp.exp2(s - m)
        l = jnp.sum(p, axis=-1, keepdims=True)
        o = jnp.dot(p.astype(BF16), v, preferred_element_type=F32)
        outs.append(o / l)
    lane = lax.broadcasted_iota(jnp.int32, outs[0].shape, 1)
    o_ref[0] = jnp.where(lane < V_HEAD_DIM, outs[0], outs[1]).astype(BF16)


def _mix_kernel(u_ref, uprev_ref, unext_ref, attn_ref, x_ref, pw_ref, ps_ref, wo_ref, g_ref, b_ref,
                out_ref, ext_ref, *, seq, alpha):
    i = pl.program_id(1)
    n = pl.num_programs(1)
    tm = u_ref.shape[1]
    ext_ref[0:POOL_HALO] = jnp.where(i > 0, uprev_ref[0], 0.0)
    ext_ref[POOL_HALO:POOL_HALO + tm] = u_ref[0]
    ext_ref[POOL_HALO + tm:] = jnp.where(i < n - 1, unext_ref[0], 0.0)
    row = i * tm + lax.broadcasted_iota(jnp.int32, (tm, 1), 0)
    acc = jnp.zeros((tm, D_MODEL), F32)
    for g, w in enumerate(POOL_WINDOWS):
        cols = slice(g * POOL_GROUP_DIM, (g + 1) * POOL_GROUP_DIM)
        back = w // 2
        win = ext_ref[pl.ds(POOL_HALO - back, tm), cols]
        for d in range(1 - back, w - back):
            win = win + ext_ref[pl.ds(POOL_HALO + d, tm), cols]
        cnt = (jnp.minimum(row + (w - back), seq) - jnp.maximum(row - back, 0)).astype(F32)
        pg = win / cnt - ext_ref[pl.ds(POOL_HALO, tm), cols]
        og = jnp.dot(pg.astype(BF16), pw_ref[g], preferred_element_type=F32) * ps_ref[:, cols]
        acc = acc + jnp.dot(og.astype(BF16), wo_ref[cols, :], preferred_element_type=F32)
    acc = acc + jnp.dot(attn_ref[0], wo_ref[POOL_WIDTH:, :], preferred_element_type=F32)
    y = alpha * x_ref[0] + acc
    out_ref[0] = _layer_norm(y, g_ref[...], b_ref[...])


def _ffn_kernel(x_ref, wg_ref, wu_ref, wd_ref, g_ref, b_ref, out_ref, *, alpha):
    x = x_ref[...]
    xb = x.astype(BF16)
    gate = jnp.dot(xb, wg_ref[...], preferred_element_type=F32)
    up = jnp.dot(xb, wu_ref[...], preferred_element_type=F32)
    hid = gate * (1.0 / (1.0 + jnp.exp(-gate))) * up
    ffn = jnp.dot(hid.astype(BF16), wd_ref[...], preferred_element_type=F32)
    y = alpha * x + ffn
    out_ref[...] = _layer_norm(y, g_ref[...], b_ref[...])


def _full(shape):
    return pl.BlockSpec(shape, lambda *_: (0,) * len(shape))


def _params(sem):
    return pltpu.CompilerParams(dimension_semantics=sem, vmem_limit_bytes=VMEM_LIMIT)


def _pad_heads(w, dim):
    r = w.shape[0]
    w = w.reshape(r, N_HEADS, dim)
    return jnp.pad(w, ((0, 0), (0, 0), (0, HEAD_SLOT - dim))).reshape(r, N_HEADS * HEAD_SLOT)


def _layer(x, pos3, invf, w_in, pool_w, pool_scale, q_norm_g, w_q_up, kv_norm_g, w_k_up, w_v_up,
           w_o, ln1_g, ln1_b, w_gate, w_up, w_down, ln2_g, ln2_b, alpha):
    bsz, seq, _ = x.shape
    tm = 512
    tq = 256
    o_kr = POOL_WIDTH + Q_LORA_RANK + KV_LORA_RANK

    w_in_ext = jnp.concatenate(
        [w_in[:, :o_kr],
         jnp.pad(w_in[:, o_kr:], ((0, 0), (ROPE_LO, LANES - ROPE_LO - QK_ROPE_DIM)))], axis=1).astype(BF16)
    wq = _pad_heads(w_q_up, QK_HEAD_DIM).astype(BF16)
    wk = _pad_heads(w_k_up, QK_NOPE_DIM).astype(BF16)
    wv = w_v_up.astype(BF16)
    q_scale = (QK_HEAD_DIM ** -0.5) * math.log2(math.e)

    tok = lambda w: pl.BlockSpec((1, tm, w), lambda b, i: (b, i, 0))
    u, q, k, v = pl.pallas_call(
        functools.partial(_proj_kernel, q_scale=q_scale),
        grid=(bsz, seq // tm),
        in_specs=[tok(D_MODEL), tok(1), _full((1, LANES)), _full((D_MODEL, H_WIDTH)),
                  _full((1, Q_LORA_RANK)), _full((Q_LORA_RANK, N_HEADS * HEAD_SLOT)),
                  _full((1, KV_LORA_RANK)), _full((KV_LORA_RANK, N_HEADS * HEAD_SLOT)),
                  _full((KV_LORA_RANK, ATTN_WIDTH))],
        out_specs=[tok(POOL_WIDTH), tok(N_HEADS * HEAD_SLOT), tok(N_HEADS * HEAD_SLOT), tok(ATTN_WIDTH)],
        out_shape=[jax.ShapeDtypeStruct((bsz, seq, POOL_WIDTH), F32),
                   jax.ShapeDtypeStruct((bsz, seq, N_HEADS * HEAD_SLOT), BF16),
                   jax.ShapeDtypeStruct((bsz, seq, N_HEADS * HEAD_SLOT), BF16),
                   jax.ShapeDtypeStruct((bsz, seq, ATTN_WIDTH), BF16)],
        compiler_params=_params(("parallel", "parallel")),
        name="proj",
    )(x, pos3, invf, w_in_ext, q_norm_g[None, :], wq, kv_norm_g[None, :], wk, wv)

    attn = pl.pallas_call(
        _attn_kernel,
        grid=(bsz, N_HEADS // 2, seq // tq),
        in_specs=[pl.BlockSpec((1, tq, 2 * HEAD_SLOT), lambda b, h, i: (b, i, h)),
                  pl.BlockSpec((1, seq, 2 * HEAD_SLOT), lambda b, h, i: (b, 0, h)),
                  pl.BlockSpec((1, seq, 2 * V_HEAD_DIM), lambda b, h, i: (b, 0, h))],
        out_specs=pl.BlockSpec((1, tq, 2 * V_HEAD_DIM), lambda b, h, i: (b, i, h)),
        out_shape=jax.ShapeDtypeStruct((bsz, seq, ATTN_WIDTH), BF16),
        compiler_params=_params(("parallel", "parallel", "arbitrary")),
        name="attn",
    )(q, k, v)

    halo_blocks = tm // POOL_HALO
    last_halo = seq // POOL_HALO - 1
    x1 = pl.pallas_call(
        functools.partial(_mix_kernel, seq=seq, alpha=alpha),
        grid=(bsz, seq // tm),
        in_specs=[tok(POOL_WIDTH),
                  pl.BlockSpec((1, POOL_HALO, POOL_WIDTH),
                               lambda b, i: (b, jnp.maximum(i * halo_blocks - 1, 0), 0)),
                  pl.BlockSpec((1, POOL_HALO, POOL_WIDTH),
                               lambda b, i: (b, jnp.minimum((i + 1) * halo_blocks, last_halo), 0)),
                  tok(ATTN_WIDTH), tok(D_MODEL),
                  _full((len(POOL_WINDOWS), POOL_GROUP_DIM, POOL_GROUP_DIM)), _full((1, POOL_WIDTH)),
                  _full((D_MODEL, D_MODEL)), _full((1, D_MODEL)), _full((1, D_MODEL))],
        out_specs=tok(D_MODEL),
        out_shape=jax.ShapeDtypeStruct((bsz, seq, D_MODEL), F32),
        scratch_shapes=[pltpu.VMEM((tm + 2 * POOL_HALO, POOL_WIDTH), F32)],
        compiler_params=_params(("parallel", "arbitrary")),
        name="mix",
    )(u, u, u, attn, x, pool_w.astype(BF16), pool_scale[None, :], w_o.astype(BF16),
      ln1_g[None, :], ln1_b[None, :])

    tf = 256
    rows = bsz * seq
    wspec = lambda shape: pl.BlockSpec(shape, lambda i: (0, 0), pipeline_mode=pl.Buffered(1))
    x2 = pl.pallas_call(
        functools.partial(_ffn_kernel, alpha=alpha),
        grid=(rows // tf,),
        in_specs=[pl.BlockSpec((tf, D_MODEL), lambda i: (i, 0)),
                  wspec((D_MODEL, FFN_HIDDEN)), wspec((D_MODEL, FFN_HIDDEN)), wspec((FFN_HIDDEN, D_MODEL)),
                  _full((1, D_MODEL)), _full((1, D_MODEL))],
        out_specs=pl.BlockSpec((tf, D_MODEL), lambda i: (i, 0)),
        out_shape=jax.ShapeDtypeStruct((rows, D_MODEL), F32),
        compiler_params=_params(("parallel",)),
        name="ffn",
    )(x1.reshape(rows, D_MODEL), w_gate.astype(BF16), w_up.astype(BF16), w_down.astype(BF16),
      ln2_g[None, :], ln2_b[None, :])
    return x2.reshape(bsz, seq, D_MODEL)


def kernel(x, positions, w_in, pool_w, pool_scale, q_norm_g, w_q_up, kv_norm_g, w_k_up, w_v_up, w_o, ln1_g, ln1_b, w_gate, w_up, w_down, ln2_g, ln2_b):
    depth = w_in.shape[0]
    alpha = (2.0 * depth) ** 0.25
    inv_freq = 1.0 / (ROPE_THETA ** (jnp.arange(0, QK_ROPE_DIM, 2, dtype=F32) / QK_ROPE_DIM))
    invf = jnp.pad(jnp.concatenate([inv_freq, inv_freq]), (ROPE_LO, LANES - ROPE_LO - QK_ROPE_DIM))[None, :]
    pos3 = positions[:, :, None]
    for l in range(depth):
        x = _layer(x, pos3, invf, w_in[l], pool_w[l], pool_scale[l], q_norm_g[l], w_q_up[l],
                   kv_norm_g[l], w_k_up[l], w_v_up[l], w_o[l], ln1_g[l], ln1_b[l],
                   w_gate[l], w_up[l], w_down[l], ln2_g[l], ln2_b[l], alpha)
    return x
```

```python
import functools
import math

import jax
import jax.numpy as jnp
from jax import lax
from jax.experimental import pallas as pl
from jax.experimental.pallas import tpu as pltpu

D_MODEL = 1024
POOL_WIDTH = 512
POOL_WINDOWS = (2, 4, 8, 16)
POOL_GROUP_DIM = 128
N_HEADS = 8
QK_NOPE_DIM = 64
QK_ROPE_DIM = 32
V_HEAD_DIM = 64
QK_HEAD_DIM = QK_NOPE_DIM + QK_ROPE_DIM
ATTN_WIDTH = N_HEADS * V_HEAD_DIM
Q_LORA_RANK = 384
KV_LORA_RANK = 256
ROPE_THETA = 10000.0
FFN_HIDDEN = 2816
LN_EPS = 1e-5
RMS_EPS = 1e-6

LANES = 128
HEAD_SLOT = LANES
ROPE_LO = QK_NOPE_DIM
ROPE_HALF = QK_ROPE_DIM // 2
H_WIDTH = POOL_WIDTH + Q_LORA_RANK + KV_LORA_RANK + LANES
POOL_HALO = 8
VMEM_LIMIT = 56 * 1024 * 1024

BF16 = jnp.bfloat16
F32 = jnp.float32


def _rope(t, cos, sin, lane):
    fwd = pltpu.roll(t, ROPE_HALF, 1)
    bwd = pltpu.roll(t, LANES - ROPE_HALF, 1)
    rot = jnp.where(lane < ROPE_LO + ROPE_HALF, -bwd, fwd)
    return t * cos + rot * sin


def _rms(t, g):
    y = t * lax.rsqrt(jnp.mean(t * t, axis=-1, keepdims=True) + RMS_EPS)
    return y * g


def _layer_norm(y, g, b):
    mu = jnp.mean(y, axis=-1, keepdims=True)
    d = y - mu
    var = jnp.mean(d * d, axis=-1, keepdims=True)
    return d * lax.rsqrt(var + LN_EPS) * g + b


def _proj_kernel(x_ref, pos_ref, invf_ref, w_in_ref, qg_ref, wq_ref, kvg_ref, wk_ref, wv_ref,
                 u_ref, q_ref, k_ref, v_ref, *, q_scale):
    x = x_ref[0].astype(BF16)
    h = jnp.dot(x, w_in_ref[...], preferred_element_type=F32)
    u_ref[0] = h[:, :POOL_WIDTH]
    o_kv = POOL_WIDTH + Q_LORA_RANK
    o_kr = o_kv + KV_LORA_RANK
    cq = _rms(h[:, POOL_WIDTH:o_kv], qg_ref[...])
    ckv = _rms(h[:, o_kv:o_kr], kvg_ref[...])
    kr = h[:, o_kr:]

    ang = pos_ref[0].astype(F32) * invf_ref[...]
    cos = jnp.cos(ang)
    sin = jnp.sin(ang)
    lane = lax.broadcasted_iota(jnp.int32, ang.shape, 1)

    q = jnp.dot(cq.astype(BF16), wq_ref[...], preferred_element_type=F32)
    ckv_b = ckv.astype(BF16)
    k_nope = jnp.dot(ckv_b, wk_ref[...], preferred_element_type=F32)
    v = jnp.dot(ckv_b, wv_ref[...], preferred_element_type=F32)
    v_ref[0] = v.astype(BF16)

    k_rope = _rope(kr, cos, sin, lane)
    for hd in range(N_HEADS):
        sl = slice(hd * HEAD_SLOT, (hd + 1) * HEAD_SLOT)
        q_ref[0, :, sl] = (_rope(q[:, sl], cos, sin, lane) * q_scale).astype(BF16)
        k_ref[0, :, sl] = (k_nope[:, sl] + k_rope).astype(BF16)


def _lane_blocks(t, op):
    out = t[:, :LANES]
    for b in range(1, t.shape[1] // LANES):
        out = op(out, t[:, b * LANES:(b + 1) * LANES])
    return out


def _attn_kernel(q_ref, k_ref, v_ref, o_ref, s_a, s_b, *, kc):
    seq = k_ref.shape[1]
    hp = k_ref.shape[2] // HEAD_SLOT
    n_chunks = seq // kc
    nt_dims = (((1,), (1,)), ((), ()))
    bufs = (s_a, s_b)

    def score_chunk(h, c, mx):
        hs = slice(h * HEAD_SLOT, (h + 1) * HEAD_SLOT)
        ks = slice(c * kc, (c + 1) * kc)
        s = lax.dot_general(q_ref[0, :, hs], k_ref[0, ks, hs], nt_dims,
                            preferred_element_type=F32)
        bufs[h % 2][:, ks] = s
        cm = _lane_blocks(s, jnp.maximum)
        return cm if mx is None else jnp.maximum(mx, cm)

    mx = None
    for c in range(n_chunks):
        mx = score_chunk(0, c, mx)
    outs = []
    for h in range(hp):
        m = jnp.max(mx, axis=-1, keepdims=True)
        vs = slice((h // 2) * LANES, (h // 2 + 1) * LANES)
        mx = lsum = acc = None
        for c in range(n_chunks):
            ks = slice(c * kc, (c + 1) * kc)
            if h + 1 < hp:
                mx = score_chunk(h + 1, c, mx)
            p = jnp.exp2(bufs[h % 2][:, ks] - m)
            ps = _lane_blocks(p, jnp.add)
            lsum = ps if lsum is None else lsum + ps
            o = jnp.dot(p.astype(BF16), v_ref[0, ks, vs], preferred_element_type=F32)
            acc = o if acc is None else acc + o
        outs.append(acc / jnp.sum(lsum, axis=-1, keepdims=True))
    lane = lax.broadcasted_iota(jnp.int32, outs[0].shape, 1)
    for j in range(hp // 2):
        o_ref[0, :, j * LANES:(j + 1) * LANES] = jnp.where(
            lane < V_HEAD_DIM, outs[2 * j], outs[2 * j + 1]).astype(BF16)


def _mix_kernel(u_ref, uprev_ref, unext_ref, attn_ref, x_ref, pw_ref, ps_ref, wo_ref, g_ref, b_ref,
                out_ref, ext_ref, *, seq, alpha):
    i = pl.program_id(1)
    n = pl.num_programs(1)
    tm = u_ref.shape[1]
    ext_ref[0:POOL_HALO] = jnp.where(i > 0, uprev_ref[0], 0.0)
    ext_ref[POOL_HALO:POOL_HALO + tm] = u_ref[0]
    ext_ref[POOL_HALO + tm:] = jnp.where(i < n - 1, unext_ref[0], 0.0)
    row = i * tm + lax.broadcasted_iota(jnp.int32, (tm, 1), 0)
    acc = jnp.zeros((tm, D_MODEL), F32)
    for g, w in enumerate(POOL_WINDOWS):
        cols = slice(g * POOL_GROUP_DIM, (g + 1) * POOL_GROUP_DIM)
        back = w // 2
        win = ext_ref[pl.ds(POOL_HALO - back, tm), cols]
        for d in range(1 - back, w - back):
            win = win + ext_ref[pl.ds(POOL_HALO + d, tm), cols]
        cnt = (jnp.minimum(row + (w - back), seq) - jnp.maximum(row - back, 0)).astype(F32)
        pg = win / cnt - ext_ref[pl.ds(POOL_HALO, tm), cols]
        og = jnp.dot(pg.astype(BF16), pw_ref[g], preferred_element_type=F32) * ps_ref[:, cols]
        acc = acc + jnp.dot(og.astype(BF16), wo_ref[cols, :], preferred_element_type=F32)
    acc = acc + jnp.dot(attn_ref[0], wo_ref[POOL_WIDTH:, :], preferred_element_type=F32)
    y = alpha * x_ref[0] + acc
    out_ref[0] = _layer_norm(y, g_ref[...], b_ref[...])


def _ffn_kernel(x_ref, wg_ref, wu_ref, wd_ref, g_ref, b_ref, out_ref, *, alpha):
    x = x_ref[...]
    xb = x.astype(BF16)
    gate = jnp.dot(xb, wg_ref[...], preferred_element_type=F32)
    up = jnp.dot(xb, wu_ref[...], preferred_element_type=F32)
    hid = gate * (1.0 / (1.0 + jnp.exp(-gate))) * up
    ffn = jnp.dot(hid.astype(BF16), wd_ref[...], preferred_element_type=F32)
    y = alpha * x + ffn
    out_ref[...] = _layer_norm(y, g_ref[...], b_ref[...])


def _full(shape):
    return pl.BlockSpec(shape, lambda *_: (0,) * len(shape))


def _params(sem):
    return pltpu.CompilerParams(dimension_semantics=sem, vmem_limit_bytes=VMEM_LIMIT)


def _pad_heads(w, dim):
    r = w.shape[0]
    w = w.reshape(r, N_HEADS, dim)
    return jnp.pad(w, ((0, 0), (0, 0), (0, HEAD_SLOT - dim))).reshape(r, N_HEADS * HEAD_SLOT)


def _layer(x, pos3, invf, w_in, pool_w, pool_scale, q_norm_g, w_q_up, kv_norm_g, w_k_up, w_v_up,
           w_o, ln1_g, ln1_b, w_gate, w_up, w_down, ln2_g, ln2_b, alpha):
    bsz, seq, _ = x.shape
    tm = 512
    tq = 256
    o_kr = POOL_WIDTH + Q_LORA_RANK + KV_LORA_RANK

    w_in_ext = jnp.concatenate(
        [w_in[:, :o_kr],
         jnp.pad(w_in[:, o_kr:], ((0, 0), (ROPE_LO, LANES - ROPE_LO - QK_ROPE_DIM)))], axis=1).astype(BF16)
    wq = _pad_heads(w_q_up, QK_HEAD_DIM).astype(BF16)
    wk = _pad_heads(w_k_up, QK_NOPE_DIM).astype(BF16)
    wv = w_v_up.astype(BF16)
    q_scale = (QK_HEAD_DIM ** -0.5) * math.log2(math.e)

    tok = lambda w: pl.BlockSpec((1, tm, w), lambda b, i: (b, i, 0))
    u, q, k, v = pl.pallas_call(
        functools.partial(_proj_kernel, q_scale=q_scale),
        grid=(bsz, seq // tm),
        in_specs=[tok(D_MODEL), tok(1), _full((1, LANES)), _full((D_MODEL, H_WIDTH)),
                  _full((1, Q_LORA_RANK)), _full((Q_LORA_RANK, N_HEADS * HEAD_SLOT)),
                  _full((1, KV_LORA_RANK)), _full((KV_LORA_RANK, N_HEADS * HEAD_SLOT)),
                  _full((KV_LORA_RANK, ATTN_WIDTH))],
        out_specs=[tok(POOL_WIDTH), tok(N_HEADS * HEAD_SLOT), tok(N_HEADS * HEAD_SLOT), tok(ATTN_WIDTH)],
        out_shape=[jax.ShapeDtypeStruct((bsz, seq, POOL_WIDTH), F32),
                   jax.ShapeDtypeStruct((bsz, seq, N_HEADS * HEAD_SLOT), BF16),
                   jax.ShapeDtypeStruct((bsz, seq, N_HEADS * HEAD_SLOT), BF16),
                   jax.ShapeDtypeStruct((bsz, seq, ATTN_WIDTH), BF16)],
        compiler_params=_params(("parallel", "parallel")),
        name="proj",
    )(x, pos3, invf, w_in_ext, q_norm_g[None, :], wq, kv_norm_g[None, :], wk, wv)

    hp = 4
    attn = pl.pallas_call(
        functools.partial(_attn_kernel, kc=512),
        grid=(bsz, N_HEADS // hp, seq // tq),
        in_specs=[pl.BlockSpec((1, tq, hp * HEAD_SLOT), lambda b, g, i: (b, i, g)),
                  pl.BlockSpec((1, seq, hp * HEAD_SLOT), lambda b, g, i: (b, 0, g)),
                  pl.BlockSpec((1, seq, hp * V_HEAD_DIM), lambda b, g, i: (b, 0, g))],
        out_specs=pl.BlockSpec((1, tq, hp * V_HEAD_DIM), lambda b, g, i: (b, i, g)),
        out_shape=jax.ShapeDtypeStruct((bsz, seq, ATTN_WIDTH), BF16),
        scratch_shapes=[pltpu.VMEM((tq, seq), F32), pltpu.VMEM((tq, seq), F32)],
        compiler_params=_params(("parallel", "parallel", "arbitrary")),
        name="attn",
    )(q, k, v)

    halo_blocks = tm // POOL_HALO
    last_halo = seq // POOL_HALO - 1
    x1 = pl.pallas_call(
        functools.partial(_mix_kernel, seq=seq, alpha=alpha),
        grid=(bsz, seq // tm),
        in_specs=[tok(POOL_WIDTH),
                  pl.BlockSpec((1, POOL_HALO, POOL_WIDTH),
                               lambda b, i: (b, jnp.maximum(i * halo_blocks - 1, 0), 0)),
                  pl.BlockSpec((1, POOL_HALO, POOL_WIDTH),
                               lambda b, i: (b, jnp.minimum((i + 1) * halo_blocks, last_halo), 0)),
                  tok(ATTN_WIDTH), tok(D_MODEL),
                  _full((len(POOL_WINDOWS), POOL_GROUP_DIM, POOL_GROUP_DIM)), _full((1, POOL_WIDTH)),
                  _full((D_MODEL, D_MODEL)), _full((1, D_MODEL)), _full((1, D_MODEL))],
        out_specs=tok(D_MODEL),
        out_shape=jax.ShapeDtypeStruct((bsz, seq, D_MODEL), F32),
        scratch_shapes=[pltpu.VMEM((tm + 2 * POOL_HALO, POOL_WIDTH), F32)],
        compiler_params=_params(("parallel", "arbitrary")),
        name="mix",
    )(u, u, u, attn, x, pool_w.astype(BF16), pool_scale[None, :], w_o.astype(BF16),
      ln1_g[None, :], ln1_b[None, :])

    tf = 256
    rows = bsz * seq
    wspec = lambda shape: pl.BlockSpec(shape, lambda i: (0, 0), pipeline_mode=pl.Buffered(1))
    x2 = pl.pallas_call(
        functools.partial(_ffn_kernel, alpha=alpha),
        grid=(rows // tf,),
        in_specs=[pl.BlockSpec((tf, D_MODEL), lambda i: (i, 0)),
                  wspec((D_MODEL, FFN_HIDDEN)), wspec((D_MODEL, FFN_HIDDEN)), wspec((FFN_HIDDEN, D_MODEL)),
                  _full((1, D_MODEL)), _full((1, D_MODEL))],
        out_specs=pl.BlockSpec((tf, D_MODEL), lambda i: (i, 0)),
        out_shape=jax.ShapeDtypeStruct((rows, D_MODEL), F32),
        compiler_params=_params(("parallel",)),
        name="ffn",
    )(x1.reshape(rows, D_MODEL), w_gate.astype(BF16), w_up.astype(BF16), w_down.astype(BF16),
      ln2_g[None, :], ln2_b[None, :])
    return x2.reshape(bsz, seq, D_MODEL)


def kernel(x, positions, w_in, pool_w, pool_scale, q_norm_g, w_q_up, kv_norm_g, w_k_up, w_v_up, w_o, ln1_g, ln1_b, w_gate, w_up, w_down, ln2_g, ln2_b):
    depth = w_in.shape[0]
    alpha = (2.0 * depth) ** 0.25
    inv_freq = 1.0 / (ROPE_THETA ** (jnp.arange(0, QK_ROPE_DIM, 2, dtype=F32) / QK_ROPE_DIM))
    invf = jnp.pad(jnp.concatenate([inv_freq, inv_freq]), (ROPE_LO, LANES - ROPE_LO - QK_ROPE_DIM))[None, :]
    pos3 = positions[:, :, None]
    for l in range(depth):
        x = _layer(x, pos3, invf, w_in[l], pool_w[l], pool_scale[l], q_norm_g[l], w_q_up[l],
                   kv_norm_g[l], w_k_up[l], w_v_up[l], w_o[l], ln1_g[l], ln1_b[l],
                   w_gate[l], w_up[l], w_down[l], ln2_g[l], ln2_b[l], alpha)
    return x
```

```python
import functools
import math

import jax
import jax.numpy as jnp
from jax import lax
from jax.experimental import pallas as pl
from jax.experimental.pallas import tpu as pltpu

D_MODEL = 1024
POOL_WIDTH = 512
POOL_WINDOWS = (2, 4, 8, 16)
POOL_GROUP_DIM = 128
N_HEADS = 8
QK_NOPE_DIM = 64
QK_ROPE_DIM = 32
V_HEAD_DIM = 64
QK_HEAD_DIM = QK_NOPE_DIM + QK_ROPE_DIM
ATTN_WIDTH = N_HEADS * V_HEAD_DIM
Q_LORA_RANK = 384
KV_LORA_RANK = 256
ROPE_THETA = 10000.0
FFN_HIDDEN = 2816
LN_EPS = 1e-5
RMS_EPS = 1e-6

LANES = 128
SUBLANES = 8
HEAD_SLOT = LANES
ROPE_LO = QK_NOPE_DIM
ROPE_HALF = QK_ROPE_DIM // 2
H_WIDTH = POOL_WIDTH + Q_LORA_RANK + KV_LORA_RANK + LANES
POOL_HALO = 8
VMEM_LIMIT = 56 * 1024 * 1024

BF16 = jnp.bfloat16
F32 = jnp.float32


def _rope(t, cos, sin, lane):
    fwd = pltpu.roll(t, ROPE_HALF, 1)
    bwd = pltpu.roll(t, LANES - ROPE_HALF, 1)
    rot = jnp.where(lane < ROPE_LO + ROPE_HALF, -bwd, fwd)
    return t * cos + rot * sin


def _rms(t, g):
    y = t * lax.rsqrt(jnp.mean(t * t, axis=-1, keepdims=True) + RMS_EPS)
    return y * g


def _layer_norm(y, g, b):
    mu = jnp.mean(y, axis=-1, keepdims=True)
    d = y - mu
    var = jnp.mean(d * d, axis=-1, keepdims=True)
    return d * lax.rsqrt(var + LN_EPS) * g + b


def _proj_kernel(x_ref, pos_ref, invf_ref, w_in_ref, qg_ref, wq_ref, kvg_ref, wk_ref, wv_ref,
                 u_ref, q_ref, k_ref, v_ref, *, q_scale):
    x = x_ref[0].astype(BF16)
    h = jnp.dot(x, w_in_ref[...], preferred_element_type=F32)
    u_ref[0] = h[:, :POOL_WIDTH]
    o_kv = POOL_WIDTH + Q_LORA_RANK
    o_kr = o_kv + KV_LORA_RANK
    cq = _rms(h[:, POOL_WIDTH:o_kv], qg_ref[...])
    ckv = _rms(h[:, o_kv:o_kr], kvg_ref[...])
    kr = h[:, o_kr:]

    ang = pos_ref[0].astype(F32) * invf_ref[...]
    cos = jnp.cos(ang)
    sin = jnp.sin(ang)
    lane = lax.broadcasted_iota(jnp.int32, ang.shape, 1)

    q = jnp.dot(cq.astype(BF16), wq_ref[...], preferred_element_type=F32)
    ckv_b = ckv.astype(BF16)
    k_nope = jnp.dot(ckv_b, wk_ref[...], preferred_element_type=F32)
    vt = lax.dot_general(wv_ref[...], ckv_b, (((1,), (1,)), ((), ())), preferred_element_type=F32)
    v_ref[0] = vt.astype(BF16)

    k_rope = _rope(kr, cos, sin, lane)
    for hd in range(N_HEADS):
        sl = slice(hd * HEAD_SLOT, (hd + 1) * HEAD_SLOT)
        q_ref[0, :, sl] = (_rope(q[:, sl], cos, sin, lane) * q_scale).astype(BF16)
        k_ref[0, :, sl] = (k_nope[:, sl] + k_rope).astype(BF16)


def _row_groups(t, op):
    return op(t.reshape(t.shape[0] // SUBLANES, SUBLANES, t.shape[1]), axis=0)


def _attn_kernel(q_ref, k_ref, vt_ref, o_ref, s_a, s_b, *, kc):
    seq = k_ref.shape[1]
    hp = k_ref.shape[2] // HEAD_SLOT
    n_chunks = seq // kc
    nt_dims = (((1,), (1,)), ((), ()))
    bufs = (s_a, s_b)

    def score_chunk(h, c, mx):
        hs = slice(h * HEAD_SLOT, (h + 1) * HEAD_SLOT)
        ks = slice(c * kc, (c + 1) * kc)
        s = lax.dot_general(k_ref[0, ks, hs], q_ref[0, :, hs], nt_dims,
                            preferred_element_type=F32)
        bufs[h % 2][ks, :] = s
        cm = _row_groups(s, jnp.max)
        return cm if mx is None else jnp.maximum(mx, cm)

    mx = None
    for c in range(n_chunks):
        mx = score_chunk(0, c, mx)
    outs = []
    for h in range(hp):
        m = jnp.max(mx, axis=0, keepdims=True)
        ds = slice(h * V_HEAD_DIM, (h + 1) * V_HEAD_DIM)
        mx = lsum = acc = None
        for c in range(n_chunks):
            ks = slice(c * kc, (c + 1) * kc)
            if h + 1 < hp:
                mx = score_chunk(h + 1, c, mx)
            p = jnp.exp2(bufs[h % 2][ks, :] - m)
            ps = _row_groups(p, jnp.sum)
            lsum = ps if lsum is None else lsum + ps
            o = jnp.dot(vt_ref[0, ds, ks], p.astype(BF16), preferred_element_type=F32)
            acc = o if acc is None else acc + o
        outs.append(acc / jnp.sum(lsum, axis=0, keepdims=True))
    o_ref[0] = jnp.concatenate(outs, axis=0).T.astype(BF16)


def _mix_kernel(u_ref, uprev_ref, unext_ref, attn_ref, x_ref, pw_ref, ps_ref, wo_ref, g_ref, b_ref,
                out_ref, ext_ref, *, seq, alpha):
    i = pl.program_id(1)
    n = pl.num_programs(1)
    tm = u_ref.shape[1]
    ext_ref[0:POOL_HALO] = jnp.where(i > 0, uprev_ref[0], 0.0)
    ext_ref[POOL_HALO:POOL_HALO + tm] = u_ref[0]
    ext_ref[POOL_HALO + tm:] = jnp.where(i < n - 1, unext_ref[0], 0.0)
    row = i * tm + lax.broadcasted_iota(jnp.int32, (tm, 1), 0)
    acc = jnp.zeros((tm, D_MODEL), F32)
    for g, w in enumerate(POOL_WINDOWS):
        cols = slice(g * POOL_GROUP_DIM, (g + 1) * POOL_GROUP_DIM)
        back = w // 2
        win = ext_ref[pl.ds(POOL_HALO - back, tm), cols]
        for d in range(1 - back, w - back):
            win = win + ext_ref[pl.ds(POOL_HALO + d, tm), cols]
        cnt = (jnp.minimum(row + (w - back), seq) - jnp.maximum(row - back, 0)).astype(F32)
        pg = win / cnt - ext_ref[pl.ds(POOL_HALO, tm), cols]
        og = jnp.dot(pg.astype(BF16), pw_ref[g], preferred_element_type=F32) * ps_ref[:, cols]
        acc = acc + jnp.dot(og.astype(BF16), wo_ref[cols, :], preferred_element_type=F32)
    acc = acc + jnp.dot(attn_ref[0], wo_ref[POOL_WIDTH:, :], preferred_element_type=F32)
    y = alpha * x_ref[0] + acc
    out_ref[0] = _layer_norm(y, g_ref[...], b_ref[...])


def _ffn_kernel(x_ref, wg_ref, wu_ref, wd_ref, g_ref, b_ref, out_ref, *, alpha):
    x = x_ref[...]
    xb = x.astype(BF16)
    gate = jnp.dot(xb, wg_ref[...], preferred_element_type=F32)
    up = jnp.dot(xb, wu_ref[...], preferred_element_type=F32)
    hid = gate * (1.0 / (1.0 + jnp.exp(-gate))) * up
    ffn = jnp.dot(hid.astype(BF16), wd_ref[...], preferred_element_type=F32)
    y = alpha * x + ffn
    out_ref[...] = _layer_norm(y, g_ref[...], b_ref[...])


def _full(shape):
    return pl.BlockSpec(shape, lambda *_: (0,) * len(shape))


def _params(sem):
    return pltpu.CompilerParams(dimension_semantics=sem, vmem_limit_bytes=VMEM_LIMIT)


def _pad_heads(w, dim):
    r = w.shape[0]
    w = w.reshape(r, N_HEADS, dim)
    return jnp.pad(w, ((0, 0), (0, 0), (0, HEAD_SLOT - dim))).reshape(r, N_HEADS * HEAD_SLOT)


def _layer(x, pos3, invf, w_in, pool_w, pool_scale, q_norm_g, w_q_up, kv_norm_g, w_k_up, w_v_up,
           w_o, ln1_g, ln1_b, w_gate, w_up, w_down, ln2_g, ln2_b, alpha):
    bsz, seq, _ = x.shape
    tm = 512
    tq = 512
    o_kr = POOL_WIDTH + Q_LORA_RANK + KV_LORA_RANK

    w_in_ext = jnp.concatenate(
        [w_in[:, :o_kr],
         jnp.pad(w_in[:, o_kr:], ((0, 0), (ROPE_LO, LANES - ROPE_LO - QK_ROPE_DIM)))], axis=1).astype(BF16)
    wq = _pad_heads(w_q_up, QK_HEAD_DIM).astype(BF16)
    wk = _pad_heads(w_k_up, QK_NOPE_DIM).astype(BF16)
    wv = w_v_up.T.astype(BF16)
    q_scale = (QK_HEAD_DIM ** -0.5) * math.log2(math.e)

    tok = lambda w: pl.BlockSpec((1, tm, w), lambda b, i: (b, i, 0))
    u, q, k, v = pl.pallas_call(
        functools.partial(_proj_kernel, q_scale=q_scale),
        grid=(bsz, seq // tm),
        in_specs=[tok(D_MODEL), tok(1), _full((1, LANES)), _full((D_MODEL, H_WIDTH)),
                  _full((1, Q_LORA_RANK)), _full((Q_LORA_RANK, N_HEADS * HEAD_SLOT)),
                  _full((1, KV_LORA_RANK)), _full((KV_LORA_RANK, N_HEADS * HEAD_SLOT)),
                  _full((ATTN_WIDTH, KV_LORA_RANK))],
        out_specs=[tok(POOL_WIDTH), tok(N_HEADS * HEAD_SLOT), tok(N_HEADS * HEAD_SLOT),
                   pl.BlockSpec((1, ATTN_WIDTH, tm), lambda b, i: (b, 0, i))],
        out_shape=[jax.ShapeDtypeStruct((bsz, seq, POOL_WIDTH), F32),
                   jax.ShapeDtypeStruct((bsz, seq, N_HEADS * HEAD_SLOT), BF16),
                   jax.ShapeDtypeStruct((bsz, seq, N_HEADS * HEAD_SLOT), BF16),
                   jax.ShapeDtypeStruct((bsz, ATTN_WIDTH, seq), BF16)],
        compiler_params=_params(("parallel", "parallel")),
        name="proj",
    )(x, pos3, invf, w_in_ext, q_norm_g[None, :], wq, kv_norm_g[None, :], wk, wv)

    hp = 4
    attn = pl.pallas_call(
        functools.partial(_attn_kernel, kc=512),
        grid=(bsz, N_HEADS // hp, seq // tq),
        in_specs=[pl.BlockSpec((1, tq, hp * HEAD_SLOT), lambda b, g, i: (b, i, g)),
                  pl.BlockSpec((1, seq, hp * HEAD_SLOT), lambda b, g, i: (b, 0, g)),
                  pl.BlockSpec((1, hp * V_HEAD_DIM, seq), lambda b, g, i: (b, g, 0))],
        out_specs=pl.BlockSpec((1, tq, hp * V_HEAD_DIM), lambda b, g, i: (b, i, g)),
        out_shape=jax.ShapeDtypeStruct((bsz, seq, ATTN_WIDTH), BF16),
        scratch_shapes=[pltpu.VMEM((seq, tq), F32), pltpu.VMEM((seq, tq), F32)],
        compiler_params=_params(("parallel", "parallel", "arbitrary")),
        name="attn",
    )(q, k, v)

    halo_blocks = tm // POOL_HALO
    last_halo = seq // POOL_HALO - 1
    x1 = pl.pallas_call(
        functools.partial(_mix_kernel, seq=seq, alpha=alpha),
        grid=(bsz, seq // tm),
        in_specs=[tok(POOL_WIDTH),
                  pl.BlockSpec((1, POOL_HALO, POOL_WIDTH),
                               lambda b, i: (b, jnp.maximum(i * halo_blocks - 1, 0), 0)),
                  pl.BlockSpec((1, POOL_HALO, POOL_WIDTH),
                               lambda b, i: (b, jnp.minimum((i + 1) * halo_blocks, last_halo), 0)),
                  tok(ATTN_WIDTH), tok(D_MODEL),
                  _full((len(POOL_WINDOWS), POOL_GROUP_DIM, POOL_GROUP_DIM)), _full((1, POOL_WIDTH)),
                  _full((D_MODEL, D_MODEL)), _full((1, D_MODEL)), _full((1, D_MODEL))],
        out_specs=tok(D_MODEL),
        out_shape=jax.ShapeDtypeStruct((bsz, seq, D_MODEL), F32),
        scratch_shapes=[pltpu.VMEM((tm + 2 * POOL_HALO, POOL_WIDTH), F32)],
        compiler_params=_params(("parallel", "arbitrary")),
        name="mix",
    )(u, u, u, attn, x, pool_w.astype(BF16), pool_scale[None, :], w_o.astype(BF16),
      ln1_g[None, :], ln1_b[None, :])

    tf = 256
    rows = bsz * seq
    wspec = lambda shape: pl.BlockSpec(shape, lambda i: (0, 0), pipeline_mode=pl.Buffered(1))
    x2 = pl.pallas_call(
        functools.partial(_ffn_kernel, alpha=alpha),
        grid=(rows // tf,),
        in_specs=[pl.BlockSpec((tf, D_MODEL), lambda i: (i, 0)),
                  wspec((D_MODEL, FFN_HIDDEN)), wspec((D_MODEL, FFN_HIDDEN)), wspec((FFN_HIDDEN, D_MODEL)),
                  _full((1, D_MODEL)), _full((1, D_MODEL))],
        out_specs=pl.BlockSpec((tf, D_MODEL), lambda i: (i, 0)),
        out_shape=jax.ShapeDtypeStruct((rows, D_MODEL), F32),
        compiler_params=_params(("parallel",)),
        name="ffn",
    )(x1.reshape(rows, D_MODEL), w_gate.astype(BF16), w_up.astype(BF16), w_down.astype(BF16),
      ln2_g[None, :], ln2_b[None, :])
    return x2.reshape(bsz, seq, D_MODEL)


def kernel(x, positions, w_in, pool_w, pool_scale, q_norm_g, w_q_up, kv_norm_g, w_k_up, w_v_up, w_o, ln1_g, ln1_b, w_gate, w_up, w_down, ln2_g, ln2_b):
    depth = w_in.shape[0]
    alpha = (2.0 * depth) ** 0.25
    inv_freq = 1.0 / (ROPE_THETA ** (jnp.arange(0, QK_ROPE_DIM, 2, dtype=F32) / QK_ROPE_DIM))
    invf = jnp.pad(jnp.concatenate([inv_freq, inv_freq]), (ROPE_LO, LANES - ROPE_LO - QK_ROPE_DIM))[None, :]
    pos3 = positions[:, :, None]
    for l in range(depth):
        x = _layer(x, pos3, invf, w_in[l], pool_w[l], pool_scale[l], q_norm_g[l], w_q_up[l],
                   kv_norm_g[l], w_k_up[l], w_v_up[l], w_o[l], ln1_g[l], ln1_b[l],
                   w_gate[l], w_up[l], w_down[l], ln2_g[l], ln2_b[l], alpha)
    return x
```

```python
import functools
import math

import jax
import jax.numpy as jnp
from jax import lax
from jax.experimental import pallas as pl
from jax.experimental.pallas import tpu as pltpu

D_MODEL = 1024
POOL_WIDTH = 512
POOL_WINDOWS = (2, 4, 8, 16)
POOL_GROUP_DIM = 128
N_HEADS = 8
QK_NOPE_DIM = 64
QK_ROPE_DIM = 32
V_HEAD_DIM = 64
QK_HEAD_DIM = QK_NOPE_DIM + QK_ROPE_DIM
ATTN_WIDTH = N_HEADS * V_HEAD_DIM
Q_LORA_RANK = 384
KV_LORA_RANK = 256
ROPE_THETA = 10000.0
FFN_HIDDEN = 2816
LN_EPS = 1e-5
RMS_EPS = 1e-6

LANES = 128
SUBLANES = 8
BF16_ROWS = 2 * SUBLANES
V_SLOT = V_HEAD_DIM + BF16_ROWS
VT_ROWS = N_HEADS * V_SLOT
HEAD_SLOT = LANES
ROPE_LO = QK_NOPE_DIM
ROPE_HALF = QK_ROPE_DIM // 2
H_WIDTH = POOL_WIDTH + Q_LORA_RANK + KV_LORA_RANK + LANES
POOL_HALO = 8
VMEM_LIMIT = 56 * 1024 * 1024

BF16 = jnp.bfloat16
F32 = jnp.float32


def _rope(t, cos, sin, lane):
    fwd = pltpu.roll(t, ROPE_HALF, 1)
    bwd = pltpu.roll(t, LANES - ROPE_HALF, 1)
    rot = jnp.where(lane < ROPE_LO + ROPE_HALF, -bwd, fwd)
    return t * cos + rot * sin


def _rms(t, g):
    y = t * lax.rsqrt(jnp.mean(t * t, axis=-1, keepdims=True) + RMS_EPS)
    return y * g


def _layer_norm(y, g, b):
    mu = jnp.mean(y, axis=-1, keepdims=True)
    d = y - mu
    var = jnp.mean(d * d, axis=-1, keepdims=True)
    return d * lax.rsqrt(var + LN_EPS) * g + b


def _proj_kernel(x_ref, pos_ref, invf_ref, w_in_ref, qg_ref, wq_ref, kvg_ref, wk_ref, wv_ref, ones_ref,
                 u_ref, q_ref, k_ref, v_ref, *, q_scale):
    x = x_ref[0].astype(BF16)
    h = jnp.dot(x, w_in_ref[...], preferred_element_type=F32)
    u_ref[0] = h[:, :POOL_WIDTH]
    o_kv = POOL_WIDTH + Q_LORA_RANK
    o_kr = o_kv + KV_LORA_RANK
    cq = _rms(h[:, POOL_WIDTH:o_kv], qg_ref[...])
    ckv = _rms(h[:, o_kv:o_kr], kvg_ref[...])
    kr = h[:, o_kr:]

    ang = pos_ref[0].astype(F32) * invf_ref[...]
    cos = jnp.cos(ang)
    sin = jnp.sin(ang)
    lane = lax.broadcasted_iota(jnp.int32, ang.shape, 1)

    q = jnp.dot(cq.astype(BF16), wq_ref[...], preferred_element_type=F32)
    ckv_b = ckv.astype(BF16)
    k_nope = jnp.dot(ckv_b, wk_ref[...], preferred_element_type=F32)
    vt = lax.dot_general(wv_ref[...], ckv_b, (((1,), (1,)), ((), ())), preferred_element_type=F32)
    v_ref[0] = (vt + ones_ref[...]).astype(BF16)

    k_rope = _rope(kr, cos, sin, lane)
    for hd in range(N_HEADS):
        sl = slice(hd * HEAD_SLOT, (hd + 1) * HEAD_SLOT)
        q_ref[0, :, sl] = (_rope(q[:, sl], cos, sin, lane) * q_scale).astype(BF16)
        k_ref[0, :, sl] = (k_nope[:, sl] + k_rope).astype(BF16)


def _row_groups(t, op):
    return op(t.reshape(t.shape[0] // SUBLANES, SUBLANES, t.shape[1]), axis=0)


SKEW = 2


def _attn_kernel(q_ref, qn_ref, k_ref, vt_ref, zero_ref, o_ref, s_a, s_b, mx_ref, *, kc):
    seq = k_ref.shape[1]
    hp = k_ref.shape[2] // HEAD_SLOT
    n_chunks = seq // kc
    nt_dims = (((1,), (1,)), ((), ()))
    bufs = (s_a, s_b)
    assert hp % 2 == 0

    def score_chunk(q, h, c):
        hs = slice(h * HEAD_SLOT, (h + 1) * HEAD_SLOT)
        ks = slice(c * kc, (c + 1) * kc)
        s = lax.dot_general(k_ref[0, ks, hs], q[0, :, hs], nt_dims,
                            preferred_element_type=F32)
        bufs[h % 2][ks, :] = s
        return _row_groups(s, jnp.max)

    def fold(cms):
        mx = cms[0]
        for cm in cms[1:]:
            mx = jnp.maximum(mx, cm)
        return mx

    @pl.when(pl.program_id(2) == 0)
    def _():
        mx_ref[...] = fold([score_chunk(q_ref, 0, c) for c in range(n_chunks)])

    mx = mx_ref[...]
    zero_bits = zero_ref[0:1, :]
    outs = []
    for h in range(hp):
        m = jnp.max(mx, axis=0, keepdims=True)
        ds = slice(h * V_SLOT, (h + 1) * V_SLOT)
        cms = []
        acc = None
        for c in range(n_chunks):
            ks = slice(c * kc, (c + 1) * kc)
            cms.append(score_chunk(q_ref, h + 1, c) if h + 1 < hp else score_chunk(qn_ref, 0, c))
            m_c = m
            if c >= SKEW:
                dep = lax.bitcast_convert_type(cms[c - SKEW][0:1, :], jnp.int32) & zero_bits
                m_c = m + lax.bitcast_convert_type(dep, F32)
            p = jnp.exp2(bufs[h % 2][ks, :] - m_c)
            o = jnp.dot(vt_ref[0, ds, ks], p.astype(BF16), preferred_element_type=F32)
            acc = o if acc is None else acc + o
        outs.append(acc[:V_HEAD_DIM] / acc[V_HEAD_DIM:V_HEAD_DIM + 1])
        mx = fold(cms)
    mx_ref[...] = mx
    o_ref[0] = jnp.concatenate(outs, axis=0).T.astype(BF16)


def _mix_kernel(u_ref, uprev_ref, unext_ref, attn_ref, x_ref, pw_ref, ps_ref, wo_ref, g_ref, b_ref,
                out_ref, ext_ref, *, seq, alpha):
    i = pl.program_id(1)
    n = pl.num_programs(1)
    tm = u_ref.shape[1]
    ext_ref[0:POOL_HALO] = jnp.where(i > 0, uprev_ref[0], 0.0)
    ext_ref[POOL_HALO:POOL_HALO + tm] = u_ref[0]
    ext_ref[POOL_HALO + tm:] = jnp.where(i < n - 1, unext_ref[0], 0.0)
    row = i * tm + lax.broadcasted_iota(jnp.int32, (tm, 1), 0)
    acc = jnp.zeros((tm, D_MODEL), F32)
    for g, w in enumerate(POOL_WINDOWS):
        cols = slice(g * POOL_GROUP_DIM, (g + 1) * POOL_GROUP_DIM)
        back = w // 2
        win = ext_ref[pl.ds(POOL_HALO - back, tm), cols]
        for d in range(1 - back, w - back):
            win = win + ext_ref[pl.ds(POOL_HALO + d, tm), cols]
        cnt = (jnp.minimum(row + (w - back), seq) - jnp.maximum(row - back, 0)).astype(F32)
        pg = win / cnt - ext_ref[pl.ds(POOL_HALO, tm), cols]
        og = jnp.dot(pg.astype(BF16), pw_ref[g], preferred_element_type=F32) * ps_ref[:, cols]
        acc = acc + jnp.dot(og.astype(BF16), wo_ref[cols, :], preferred_element_type=F32)
    acc = acc + jnp.dot(attn_ref[0], wo_ref[POOL_WIDTH:, :], preferred_element_type=F32)
    y = alpha * x_ref[0] + acc
    out_ref[0] = _layer_norm(y, g_ref[...], b_ref[...])


def _ffn_kernel(x_ref, wg_ref, wu_ref, wd_ref, g_ref, b_ref, out_ref, *, alpha):
    x = x_ref[...]
    xb = x.astype(BF16)
    gate = jnp.dot(xb, wg_ref[...], preferred_element_type=F32)
    up = jnp.dot(xb, wu_ref[...], preferred_element_type=F32)
    hid = gate * (1.0 / (1.0 + jnp.exp(-gate))) * up
    ffn = jnp.dot(hid.astype(BF16), wd_ref[...], preferred_element_type=F32)
    y = alpha * x + ffn
    out_ref[...] = _layer_norm(y, g_ref[...], b_ref[...])


def _full(shape):
    return pl.BlockSpec(shape, lambda *_: (0,) * len(shape))


def _params(sem):
    return pltpu.CompilerParams(dimension_semantics=sem, vmem_limit_bytes=VMEM_LIMIT)


def _pad_heads(w, dim):
    r = w.shape[0]
    w = w.reshape(r, N_HEADS, dim)
    return jnp.pad(w, ((0, 0), (0, 0), (0, HEAD_SLOT - dim))).reshape(r, N_HEADS * HEAD_SLOT)


def _layer(x, pos3, invf, w_in, pool_w, pool_scale, q_norm_g, w_q_up, kv_norm_g, w_k_up, w_v_up,
           w_o, ln1_g, ln1_b, w_gate, w_up, w_down, ln2_g, ln2_b, alpha):
    bsz, seq, _ = x.shape
    tm = 512
    tq = 512
    o_kr = POOL_WIDTH + Q_LORA_RANK + KV_LORA_RANK

    w_in_ext = jnp.concatenate(
        [w_in[:, :o_kr],
         jnp.pad(w_in[:, o_kr:], ((0, 0), (ROPE_LO, LANES - ROPE_LO - QK_ROPE_DIM)))], axis=1).astype(BF16)
    wq = _pad_heads(w_q_up, QK_HEAD_DIM).astype(BF16)
    wk = _pad_heads(w_k_up, QK_NOPE_DIM).astype(BF16)
    wv = jnp.pad(w_v_up.T.reshape(N_HEADS, V_HEAD_DIM, KV_LORA_RANK),
                 ((0, 0), (0, V_SLOT - V_HEAD_DIM), (0, 0))).reshape(VT_ROWS, KV_LORA_RANK).astype(BF16)
    ones_col = (jnp.arange(VT_ROWS) % V_SLOT == V_HEAD_DIM).astype(F32)[:, None]
    q_scale = (QK_HEAD_DIM ** -0.5) * math.log2(math.e)

    tok = lambda w: pl.BlockSpec((1, tm, w), lambda b, i: (b, i, 0))
    u, q, k, v = pl.pallas_call(
        functools.partial(_proj_kernel, q_scale=q_scale),
        grid=(bsz, seq // tm),
        in_specs=[tok(D_MODEL), tok(1), _full((1, LANES)), _full((D_MODEL, H_WIDTH)),
                  _full((1, Q_LORA_RANK)), _full((Q_LORA_RANK, N_HEADS * HEAD_SLOT)),
                  _full((1, KV_LORA_RANK)), _full((KV_LORA_RANK, N_HEADS * HEAD_SLOT)),
                  _full((VT_ROWS, KV_LORA_RANK)), _full((VT_ROWS, 1))],
        out_specs=[tok(POOL_WIDTH), tok(N_HEADS * HEAD_SLOT), tok(N_HEADS * HEAD_SLOT),
                   pl.BlockSpec((1, VT_ROWS, tm), lambda b, i: (b, 0, i))],
        out_shape=[jax.ShapeDtypeStruct((bsz, seq, POOL_WIDTH), F32),
                   jax.ShapeDtypeStruct((bsz, seq, N_HEADS * HEAD_SLOT), BF16),
                   jax.ShapeDtypeStruct((bsz, seq, N_HEADS * HEAD_SLOT), BF16),
                   jax.ShapeDtypeStruct((bsz, VT_ROWS, seq), BF16)],
        compiler_params=_params(("parallel", "parallel")),
        name="proj",
    )(x, pos3, invf, w_in_ext, q_norm_g[None, :], wq, kv_norm_g[None, :], wk, wv, ones_col)

    hp = 4
    attn = pl.pallas_call(
        functools.partial(_attn_kernel, kc=512),
        grid=(bsz, N_HEADS // hp, seq // tq),
        in_specs=[pl.BlockSpec((1, tq, hp * HEAD_SLOT), lambda b, g, i: (b, i, g)),
                  pl.BlockSpec((1, tq, hp * HEAD_SLOT),
                               lambda b, g, i: (b, jnp.minimum(i + 1, seq // tq - 1), g)),
                  pl.BlockSpec((1, seq, hp * HEAD_SLOT), lambda b, g, i: (b, 0, g)),
                  pl.BlockSpec((1, hp * V_SLOT, seq), lambda b, g, i: (b, g, 0)),
                  _full((SUBLANES, tq))],
        out_specs=pl.BlockSpec((1, tq, hp * V_HEAD_DIM), lambda b, g, i: (b, i, g)),
        out_shape=jax.ShapeDtypeStruct((bsz, seq, ATTN_WIDTH), BF16),
        scratch_shapes=[pltpu.VMEM((seq, tq), F32), pltpu.VMEM((seq, tq), F32),
                        pltpu.VMEM((SUBLANES, tq), F32)],
        compiler_params=_params(("parallel", "parallel", "arbitrary")),
        name="attn",
    )(q, q, k, v, jnp.zeros((SUBLANES, tq), jnp.int32))

    halo_blocks = tm // POOL_HALO
    last_halo = seq // POOL_HALO - 1
    x1 = pl.pallas_call(
        functools.partial(_mix_kernel, seq=seq, alpha=alpha),
        grid=(bsz, seq // tm),
        in_specs=[tok(POOL_WIDTH),
                  pl.BlockSpec((1, POOL_HALO, POOL_WIDTH),
                               lambda b, i: (b, jnp.maximum(i * halo_blocks - 1, 0), 0)),
                  pl.BlockSpec((1, POOL_HALO, POOL_WIDTH),
                               lambda b, i: (b, jnp.minimum((i + 1) * halo_blocks, last_halo), 0)),
                  tok(ATTN_WIDTH), tok(D_MODEL),
                  _full((len(POOL_WINDOWS), POOL_GROUP_DIM, POOL_GROUP_DIM)), _full((1, POOL_WIDTH)),
                  _full((D_MODEL, D_MODEL)), _full((1, D_MODEL)), _full((1, D_MODEL))],
        out_specs=tok(D_MODEL),
        out_shape=jax.ShapeDtypeStruct((bsz, seq, D_MODEL), F32),
        scratch_shapes=[pltpu.VMEM((tm + 2 * POOL_HALO, POOL_WIDTH), F32)],
        compiler_params=_params(("parallel", "arbitrary")),
        name="mix",
    )(u, u, u, attn, x, pool_w.astype(BF16), pool_scale[None, :], w_o.astype(BF16),
      ln1_g[None, :], ln1_b[None, :])

    tf = 256
    rows = bsz * seq
    wspec = lambda shape: pl.BlockSpec(shape, lambda i: (0, 0), pipeline_mode=pl.Buffered(1))
    x2 = pl.pallas_call(
        functools.partial(_ffn_kernel, alpha=alpha),
        grid=(rows // tf,),
        in_specs=[pl.BlockSpec((tf, D_MODEL), lambda i: (i, 0)),
                  wspec((D_MODEL, FFN_HIDDEN)), wspec((D_MODEL, FFN_HIDDEN)), wspec((FFN_HIDDEN, D_MODEL)),
                  _full((1, D_MODEL)), _full((1, D_MODEL))],
        out_specs=pl.BlockSpec((tf, D_MODEL), lambda i: (i, 0)),
        out_shape=jax.ShapeDtypeStruct((rows, D_MODEL), F32),
        compiler_params=_params(("parallel",)),
        name="ffn",
    )(x1.reshape(rows, D_MODEL), w_gate.astype(BF16), w_up.astype(BF16), w_down.astype(BF16),
      ln2_g[None, :], ln2_b[None, :])
    return x2.reshape(bsz, seq, D_MODEL)


def kernel(x, positions, w_in, pool_w, pool_scale, q_norm_g, w_q_up, kv_norm_g, w_k_up, w_v_up, w_o, ln1_g, ln1_b, w_gate, w_up, w_down, ln2_g, ln2_b):
    depth = w_in.shape[0]
    alpha = (2.0 * depth) ** 0.25
    inv_freq = 1.0 / (ROPE_THETA ** (jnp.arange(0, QK_ROPE_DIM, 2, dtype=F32) / QK_ROPE_DIM))
    invf = jnp.pad(jnp.concatenate([inv_freq, inv_freq]), (ROPE_LO, LANES - ROPE_LO - QK_ROPE_DIM))[None, :]
    pos3 = positions[:, :, None]
    for l in range(depth):
        x = _layer(x, pos3, invf, w_in[l], pool_w[l], pool_scale[l], q_norm_g[l], w_q_up[l],
                   kv_norm_g[l], w_k_up[l], w_v_up[l], w_o[l], ln1_g[l], ln1_b[l],
                   w_gate[l], w_up[l], w_down[l], ln2_g[l], ln2_b[l], alpha)
    return x
```

```python
import functools
import math

import jax
import jax.numpy as jnp
from jax import lax
from jax.experimental import pallas as pl
from jax.experimental.pallas import tpu as pltpu

D_MODEL = 1024
POOL_WIDTH = 512
POOL_WINDOWS = (2, 4, 8, 16)
POOL_GROUP_DIM = 128
N_HEADS = 8
QK_NOPE_DIM = 64
QK_ROPE_DIM = 32
V_HEAD_DIM = 64
QK_HEAD_DIM = QK_NOPE_DIM + QK_ROPE_DIM
ATTN_WIDTH = N_HEADS * V_HEAD_DIM
Q_LORA_RANK = 384
KV_LORA_RANK = 256
ROPE_THETA = 10000.0
FFN_HIDDEN = 2816
LN_EPS = 1e-5
RMS_EPS = 1e-6

LANES = 128
SUBLANES = 8
BF16_ROWS = 2 * SUBLANES
V_SLOT = V_HEAD_DIM + BF16_ROWS
VT_ROWS = N_HEADS * V_SLOT
HEAD_SLOT = LANES
ROPE_LO = QK_NOPE_DIM
ROPE_HALF = QK_ROPE_DIM // 2
H_WIDTH = POOL_WIDTH + Q_LORA_RANK + KV_LORA_RANK + LANES
POOL_HALO = 8
VMEM_LIMIT = 56 * 1024 * 1024

BF16 = jnp.bfloat16
F32 = jnp.float32


def _rope(t, cos, sin_signed):
    partner = pltpu.roll(t, LANES - ROPE_HALF, 1)
    return t * cos + partner * sin_signed


def _rms(t, g):
    y = t * lax.rsqrt(jnp.mean(t * t, axis=-1, keepdims=True) + RMS_EPS)
    return y * g


def _layer_norm(y, g, b):
    mu = jnp.mean(y, axis=-1, keepdims=True)
    d = y - mu
    var = jnp.mean(d * d, axis=-1, keepdims=True)
    return d * lax.rsqrt(var + LN_EPS) * g + b


def _rope_table_kernel(invf_ref, pos_ref, cos_ref, sin_ref):
    pos = pos_ref[...].astype(F32)
    for f in range(ROPE_HALF):
        ang = pos * invf_ref[f]
        cos_ref[f] = jnp.cos(ang)
        sin_ref[f] = jnp.sin(ang)


def _rope_lanes(t, outside):
    tm = t.shape[1]
    full = jnp.concatenate([jnp.full((ROPE_LO, tm), outside, F32), t, t,
                            jnp.zeros((LANES - ROPE_LO - QK_ROPE_DIM, tm), F32)], axis=0)
    return full.T


def _proj_kernel(x_ref, cos_ref, sin_ref, w_in_ref, qg_ref, wq_ref, kvg_ref, wk_ref, wv_ref, ones_ref,
                 u_ref, q_ref, k_ref, v_ref, *, q_scale):
    x = x_ref[0].astype(BF16)
    h = jnp.dot(x, w_in_ref[...], preferred_element_type=F32)
    u_ref[0] = h[:, :POOL_WIDTH]
    o_kv = POOL_WIDTH + Q_LORA_RANK
    o_kr = o_kv + KV_LORA_RANK
    cq = _rms(h[:, POOL_WIDTH:o_kv], qg_ref[...])
    ckv = _rms(h[:, o_kv:o_kr], kvg_ref[...])
    kr = h[:, o_kr:]

    cos = _rope_lanes(cos_ref[...], 1.0)
    sin = _rope_lanes(sin_ref[...], 0.0)
    first_half = lax.broadcasted_iota(jnp.int32, cos.shape, 1) < ROPE_LO + ROPE_HALF
    sin = jnp.where(first_half, -sin, sin)
    cos_q = cos * q_scale
    sin_q = sin * q_scale

    q = jnp.dot(cq.astype(BF16), wq_ref[...], preferred_element_type=F32)
    ckv_b = ckv.astype(BF16)
    k_nope = jnp.dot(ckv_b, wk_ref[...], preferred_element_type=F32)
    vt = lax.dot_general(wv_ref[...], ckv_b, (((1,), (1,)), ((), ())), preferred_element_type=F32)
    v_ref[0] = (vt + ones_ref[...]).astype(BF16)

    k_rope = _rope(kr, cos, sin)
    for hd in range(N_HEADS):
        sl = slice(hd * HEAD_SLOT, (hd + 1) * HEAD_SLOT)
        q_ref[0, :, sl] = _rope(q[:, sl], cos_q, sin_q).astype(BF16)
        k_ref[0, :, sl] = (k_nope[:, sl] + k_rope).astype(BF16)


def _row_groups(t, op):
    return op(t.reshape(t.shape[0] // SUBLANES, SUBLANES, t.shape[1]), axis=0)


SKEW = 2


def _attn_kernel(q_ref, qn_ref, k_ref, vt_ref, zero_ref, o_ref, s_a, s_b, mx_ref, *, kc):
    seq = k_ref.shape[1]
    hp = k_ref.shape[2] // HEAD_SLOT
    n_chunks = seq // kc
    nt_dims = (((1,), (1,)), ((), ()))
    bufs = (s_a, s_b)
    assert hp % 2 == 0

    def score_chunk(q, h, c):
        hs = slice(h * HEAD_SLOT, (h + 1) * HEAD_SLOT)
        ks = slice(c * kc, (c + 1) * kc)
        s = lax.dot_general(k_ref[0, ks, hs], q[0, :, hs], nt_dims,
                            preferred_element_type=F32)
        bufs[h % 2][ks, :] = s
        return _row_groups(s, jnp.max)

    def fold(cms):
        mx = cms[0]
        for cm in cms[1:]:
            mx = jnp.maximum(mx, cm)
        return mx

    @pl.when(pl.program_id(2) == 0)
    def _():
        mx_ref[...] = fold([score_chunk(q_ref, 0, c) for c in range(n_chunks)])

    mx = mx_ref[...]
    zero_bits = zero_ref[0:1, :]
    outs = []
    for h in range(hp):
        m = jnp.max(mx, axis=0, keepdims=True)
        ds = slice(h * V_SLOT, (h + 1) * V_SLOT)
        cms = []
        acc = None
        for c in range(n_chunks):
            ks = slice(c * kc, (c + 1) * kc)
            cms.append(score_chunk(q_ref, h + 1, c) if h + 1 < hp else score_chunk(qn_ref, 0, c))
            m_c = m
            if c >= SKEW:
                dep = lax.bitcast_convert_type(cms[c - SKEW][0:1, :], jnp.int32) & zero_bits
                m_c = m + lax.bitcast_convert_type(dep, F32)
            p = jnp.exp2(bufs[h % 2][ks, :] - m_c)
            o = jnp.dot(vt_ref[0, ds, ks], p.astype(BF16), preferred_element_type=F32)
            acc = o if acc is None else acc + o
        outs.append(acc[:V_HEAD_DIM] / acc[V_HEAD_DIM:V_HEAD_DIM + 1])
        mx = fold(cms)
    mx_ref[...] = mx
    o_ref[0] = jnp.concatenate(outs, axis=0).T.astype(BF16)


def _mix_kernel(u_ref, uprev_ref, unext_ref, attn_ref, x_ref, pw_ref, ps_ref, wo_ref, g_ref, b_ref,
                out_ref, ext_ref, *, seq, alpha):
    i = pl.program_id(1)
    n = pl.num_programs(1)
    tm = u_ref.shape[1]
    ext_ref[0:POOL_HALO] = jnp.where(i > 0, uprev_ref[0], 0.0)
    ext_ref[POOL_HALO:POOL_HALO + tm] = u_ref[0]
    ext_ref[POOL_HALO + tm:] = jnp.where(i < n - 1, unext_ref[0], 0.0)
    row = i * tm + lax.broadcasted_iota(jnp.int32, (tm, 1), 0)
    acc = jnp.zeros((tm, D_MODEL), F32)
    for g, w in enumerate(POOL_WINDOWS):
        cols = slice(g * POOL_GROUP_DIM, (g + 1) * POOL_GROUP_DIM)
        back = w // 2
        win = ext_ref[pl.ds(POOL_HALO - back, tm), cols]
        for d in range(1 - back, w - back):
            win = win + ext_ref[pl.ds(POOL_HALO + d, tm), cols]
        cnt = (jnp.minimum(row + (w - back), seq) - jnp.maximum(row - back, 0)).astype(F32)
        pg = win / cnt - ext_ref[pl.ds(POOL_HALO, tm), cols]
        og = jnp.dot(pg.astype(BF16), pw_ref[g], preferred_element_type=F32) * ps_ref[:, cols]
        acc = acc + jnp.dot(og.astype(BF16), wo_ref[cols, :], preferred_element_type=F32)
    acc = acc + jnp.dot(attn_ref[0], wo_ref[POOL_WIDTH:, :], preferred_element_type=F32)
    y = alpha * x_ref[0] + acc
    out_ref[0] = _layer_norm(y, g_ref[...], b_ref[...])


def _ffn_kernel(x_ref, wg_ref, wu_ref, wd_ref, g_ref, b_ref, out_ref, *, alpha):
    x = x_ref[...]
    xb = x.astype(BF16)
    gate = jnp.dot(xb, wg_ref[...], preferred_element_type=F32)
    up = jnp.dot(xb, wu_ref[...], preferred_element_type=F32)
    hid = gate * (1.0 / (1.0 + jnp.exp(-gate))) * up
    ffn = jnp.dot(hid.astype(BF16), wd_ref[...], preferred_element_type=F32)
    y = alpha * x + ffn
    out_ref[...] = _layer_norm(y, g_ref[...], b_ref[...])


def _full(shape):
    return pl.BlockSpec(shape, lambda *_: (0,) * len(shape))


def _params(sem):
    return pltpu.CompilerParams(dimension_semantics=sem, vmem_limit_bytes=VMEM_LIMIT)


def _pad_heads(w, dim):
    r = w.shape[0]
    w = w.reshape(r, N_HEADS, dim)
    return jnp.pad(w, ((0, 0), (0, 0), (0, HEAD_SLOT - dim))).reshape(r, N_HEADS * HEAD_SLOT)


def _rope_slot(w):
    return jnp.concatenate([w, w[..., :ROPE_HALF], jnp.zeros_like(w[..., :ROPE_HALF])], axis=-1)


def _layer(x, cos_t, sin_t, w_in, pool_w, pool_scale, q_norm_g, w_q_up, kv_norm_g, w_k_up, w_v_up,
           w_o, ln1_g, ln1_b, w_gate, w_up, w_down, ln2_g, ln2_b, alpha):
    bsz, seq, _ = x.shape
    tm = 512
    tq = 512
    o_kr = POOL_WIDTH + Q_LORA_RANK + KV_LORA_RANK

    w_in_ext = jnp.concatenate(
        [w_in[:, :o_kr], jnp.zeros((D_MODEL, ROPE_LO), w_in.dtype), _rope_slot(w_in[:, o_kr:])],
        axis=1).astype(BF16)
    wq3 = w_q_up.reshape(Q_LORA_RANK, N_HEADS, QK_HEAD_DIM)
    wq = jnp.concatenate([wq3[..., :QK_NOPE_DIM], _rope_slot(wq3[..., QK_NOPE_DIM:])],
                         axis=-1).reshape(Q_LORA_RANK, N_HEADS * HEAD_SLOT).astype(BF16)
    wk = _pad_heads(w_k_up, QK_NOPE_DIM).astype(BF16)
    wv = jnp.pad(w_v_up.T.reshape(N_HEADS, V_HEAD_DIM, KV_LORA_RANK),
                 ((0, 0), (0, V_SLOT - V_HEAD_DIM), (0, 0))).reshape(VT_ROWS, KV_LORA_RANK).astype(BF16)
    ones_col = (jnp.arange(VT_ROWS) % V_SLOT == V_HEAD_DIM).astype(F32)[:, None]
    q_scale = (QK_HEAD_DIM ** -0.5) * math.log2(math.e)

    tok = lambda w: pl.BlockSpec((1, tm, w), lambda b, i: (b, i, 0))
    rope = lambda: pl.BlockSpec((ROPE_HALF, tm), lambda b, i: (0, b * (seq // tm) + i))
    u, q, k, v = pl.pallas_call(
        functools.partial(_proj_kernel, q_scale=q_scale),
        grid=(bsz, seq // tm),
        in_specs=[tok(D_MODEL), rope(), rope(), _full((D_MODEL, H_WIDTH)),
                  _full((1, Q_LORA_RANK)), _full((Q_LORA_RANK, N_HEADS * HEAD_SLOT)),
                  _full((1, KV_LORA_RANK)), _full((KV_LORA_RANK, N_HEADS * HEAD_SLOT)),
                  _full((VT_ROWS, KV_LORA_RANK)), _full((VT_ROWS, 1))],
        out_specs=[tok(POOL_WIDTH), tok(N_HEADS * HEAD_SLOT), tok(N_HEADS * HEAD_SLOT),
                   pl.BlockSpec((1, VT_ROWS, tm), lambda b, i: (b, 0, i))],
        out_shape=[jax.ShapeDtypeStruct((bsz, seq, POOL_WIDTH), F32),
                   jax.ShapeDtypeStruct((bsz, seq, N_HEADS * HEAD_SLOT), BF16),
                   jax.ShapeDtypeStruct((bsz, seq, N_HEADS * HEAD_SLOT), BF16),
                   jax.ShapeDtypeStruct((bsz, VT_ROWS, seq), BF16)],
        compiler_params=_params(("parallel", "parallel")),
        name="proj",
    )(x, cos_t, sin_t, w_in_ext, q_norm_g[None, :], wq, kv_norm_g[None, :], wk, wv, ones_col)

    hp = 4
    attn = pl.pallas_call(
        functools.partial(_attn_kernel, kc=512),
        grid=(bsz, N_HEADS // hp, seq // tq),
        in_specs=[pl.BlockSpec((1, tq, hp * HEAD_SLOT), lambda b, g, i: (b, i, g)),
                  pl.BlockSpec((1, tq, hp * HEAD_SLOT),
                               lambda b, g, i: (b, jnp.minimum(i + 1, seq // tq - 1), g)),
                  pl.BlockSpec((1, seq, hp * HEAD_SLOT), lambda b, g, i: (b, 0, g)),
                  pl.BlockSpec((1, hp * V_SLOT, seq), lambda b, g, i: (b, g, 0)),
                  _full((SUBLANES, tq))],
        out_specs=pl.BlockSpec((1, tq, hp * V_HEAD_DIM), lambda b, g, i: (b, i, g)),
        out_shape=jax.ShapeDtypeStruct((bsz, seq, ATTN_WIDTH), BF16),
        scratch_shapes=[pltpu.VMEM((seq, tq), F32), pltpu.VMEM((seq, tq), F32),
                        pltpu.VMEM((SUBLANES, tq), F32)],
        compiler_params=_params(("parallel", "parallel", "arbitrary")),
        name="attn",
    )(q, q, k, v, jnp.zeros((SUBLANES, tq), jnp.int32))

    halo_blocks = tm // POOL_HALO
    last_halo = seq // POOL_HALO - 1
    x1 = pl.pallas_call(
        functools.partial(_mix_kernel, seq=seq, alpha=alpha),
        grid=(bsz, seq // tm),
        in_specs=[tok(POOL_WIDTH),
                  pl.BlockSpec((1, POOL_HALO, POOL_WIDTH),
                               lambda b, i: (b, jnp.maximum(i * halo_blocks - 1, 0), 0)),
                  pl.BlockSpec((1, POOL_HALO, POOL_WIDTH),
                               lambda b, i: (b, jnp.minimum((i + 1) * halo_blocks, last_halo), 0)),
                  tok(ATTN_WIDTH), tok(D_MODEL),
                  _full((len(POOL_WINDOWS), POOL_GROUP_DIM, POOL_GROUP_DIM)), _full((1, POOL_WIDTH)),
                  _full((D_MODEL, D_MODEL)), _full((1, D_MODEL)), _full((1, D_MODEL))],
        out_specs=tok(D_MODEL),
        out_shape=jax.ShapeDtypeStruct((bsz, seq, D_MODEL), F32),
        scratch_shapes=[pltpu.VMEM((tm + 2 * POOL_HALO, POOL_WIDTH), F32)],
        compiler_params=_params(("parallel", "arbitrary")),
        name="mix",
    )(u, u, u, attn, x, pool_w.astype(BF16), pool_scale[None, :], w_o.astype(BF16),
      ln1_g[None, :], ln1_b[None, :])

    tf = 256
    rows = bsz * seq
    wspec = lambda shape: pl.BlockSpec(shape, lambda i: (0, 0), pipeline_mode=pl.Buffered(1))
    x2 = pl.pallas_call(
        functools.partial(_ffn_kernel, alpha=alpha),
        grid=(rows // tf,),
        in_specs=[pl.BlockSpec((tf, D_MODEL), lambda i: (i, 0)),
                  wspec((D_MODEL, FFN_HIDDEN)), wspec((D_MODEL, FFN_HIDDEN)), wspec((FFN_HIDDEN, D_MODEL)),
                  _full((1, D_MODEL)), _full((1, D_MODEL))],
        out_specs=pl.BlockSpec((tf, D_MODEL), lambda i: (i, 0)),
        out_shape=jax.ShapeDtypeStruct((rows, D_MODEL), F32),
        compiler_params=_params(("parallel",)),
        name="ffn",
    )(x1.reshape(rows, D_MODEL), w_gate.astype(BF16), w_up.astype(BF16), w_down.astype(BF16),
      ln2_g[None, :], ln2_b[None, :])
    return x2.reshape(bsz, seq, D_MODEL)


def kernel(x, positions, w_in, pool_w, pool_scale, q_norm_g, w_q_up, kv_norm_g, w_k_up, w_v_up, w_o, ln1_g, ln1_b, w_gate, w_up, w_down, ln2_g, ln2_b):
    depth = w_in.shape[0]
    alpha = (2.0 * depth) ** 0.25
    inv_freq = 1.0 / (ROPE_THETA ** (jnp.arange(0, QK_ROPE_DIM, 2, dtype=F32) / QK_ROPE_DIM))
    tokens = positions.size
    plane = (ROPE_HALF, tokens // LANES, LANES)
    cos_t, sin_t = pl.pallas_call(
        _rope_table_kernel,
        in_specs=[pl.BlockSpec(memory_space=pltpu.SMEM), _full(plane[1:])],
        out_specs=[_full(plane), _full(plane)],
        out_shape=[jax.ShapeDtypeStruct(plane, F32)] * 2,
        grid=(1,),
        name="rope_table",
    )(inv_freq, positions.reshape(plane[1:]))
    cos_t = cos_t.reshape(ROPE_HALF, tokens)
    sin_t = sin_t.reshape(ROPE_HALF, tokens)
    for l in range(depth):
        x = _layer(x, cos_t, sin_t, w_in[l], pool_w[l], pool_scale[l], q_norm_g[l], w_q_up[l],
                   kv_norm_g[l], w_k_up[l], w_v_up[l], w_o[l], ln1_g[l], ln1_b[l],
                   w_gate[l], w_up[l], w_down[l], ln2_g[l], ln2_b[l], alpha)
    return x
```

```python
import functools
import math

import jax
import jax.numpy as jnp
from jax import lax
from jax.experimental import pallas as pl
from jax.experimental.pallas import tpu as pltpu

D_MODEL = 1024
POOL_WIDTH = 512
POOL_WINDOWS = (2, 4, 8, 16)
POOL_GROUP_DIM = 128
N_HEADS = 8
QK_NOPE_DIM = 64
QK_ROPE_DIM = 32
V_HEAD_DIM = 64
QK_HEAD_DIM = QK_NOPE_DIM + QK_ROPE_DIM
ATTN_WIDTH = N_HEADS * V_HEAD_DIM
Q_LORA_RANK = 384
KV_LORA_RANK = 256
ROPE_THETA = 10000.0
FFN_HIDDEN = 2816
LN_EPS = 1e-5
RMS_EPS = 1e-6

LANES = 128
SUBLANES = 8
BF16_ROWS = 2 * SUBLANES
V_SLOT = V_HEAD_DIM + BF16_ROWS
VT_ROWS = N_HEADS * V_SLOT
HEAD_SLOT = LANES
ROPE_LO = QK_NOPE_DIM
ROPE_HALF = QK_ROPE_DIM // 2
H_WIDTH = POOL_WIDTH + Q_LORA_RANK + KV_LORA_RANK + LANES
POOL_HALO = 8
VMEM_LIMIT = 56 * 1024 * 1024

BF16 = jnp.bfloat16
F32 = jnp.float32


def _rope(t, cos, sin_signed):
    partner = pltpu.roll(t, LANES - ROPE_HALF, 1)
    return t * cos + partner * sin_signed


def _rms(t, g):
    y = t * lax.rsqrt(jnp.mean(t * t, axis=-1, keepdims=True) + RMS_EPS)
    return y * g


def _layer_norm(y, g, b):
    mu = jnp.mean(y, axis=-1, keepdims=True)
    d = y - mu
    var = jnp.mean(d * d, axis=-1, keepdims=True)
    return d * lax.rsqrt(var + LN_EPS) * g + b


def _rope_table_kernel(invf_ref, pos_ref, cos_ref, sin_ref):
    pos = pos_ref[...].astype(F32)
    for f in range(ROPE_HALF):
        ang = pos * invf_ref[f]
        cos_ref[f] = jnp.cos(ang)
        sin_ref[f] = jnp.sin(ang)


def _rope_lanes(t, outside):
    tm = t.shape[1]
    full = jnp.concatenate([jnp.full((ROPE_LO, tm), outside, F32), t, t,
                            jnp.zeros((LANES - ROPE_LO - QK_ROPE_DIM, tm), F32)], axis=0)
    return full.T


def _proj_kernel(x_ref, cos_ref, sin_ref, w_in_ref, qg_ref, wq_ref, kvg_ref, wk_ref, wv_ref, ones_ref,
                 u_ref, q_ref, k_ref, v_ref, *, q_scale):
    x = x_ref[0].astype(BF16)
    h = jnp.dot(x, w_in_ref[...], preferred_element_type=F32)
    u_ref[0] = h[:, :POOL_WIDTH]
    o_kv = POOL_WIDTH + Q_LORA_RANK
    o_kr = o_kv + KV_LORA_RANK
    cq = _rms(h[:, POOL_WIDTH:o_kv], qg_ref[...])
    ckv = _rms(h[:, o_kv:o_kr], kvg_ref[...])
    kr = h[:, o_kr:]

    cos = _rope_lanes(cos_ref[...], 1.0)
    sin = _rope_lanes(sin_ref[...], 0.0)
    first_half = lax.broadcasted_iota(jnp.int32, cos.shape, 1) < ROPE_LO + ROPE_HALF
    sin = jnp.where(first_half, -sin, sin)
    cos_q = cos * q_scale
    sin_q = sin * q_scale

    q = jnp.dot(cq.astype(BF16), wq_ref[...], preferred_element_type=F32)
    ckv_b = ckv.astype(BF16)
    k_nope = jnp.dot(ckv_b, wk_ref[...], preferred_element_type=F32)
    vt = lax.dot_general(wv_ref[...], ckv_b, (((1,), (1,)), ((), ())), preferred_element_type=F32)
    v_ref[0] = (vt + ones_ref[...]).astype(BF16)

    k_rope = _rope(kr, cos, sin)
    for hd in range(N_HEADS):
        sl = slice(hd * HEAD_SLOT, (hd + 1) * HEAD_SLOT)
        q_ref[0, :, sl] = _rope(q[:, sl], cos_q, sin_q).astype(BF16)
        k_ref[0, :, sl] = (k_nope[:, sl] + k_rope).astype(BF16)


def _row_groups(t, op):
    return op(t.reshape(t.shape[0] // SUBLANES, SUBLANES, t.shape[1]), axis=0)


SKEW = 4


def _attn_kernel(q_ref, qn_ref, k_ref, vt_ref, zero_ref, o_ref, s_a, s_b, mx_ref, *, kc):
    seq = k_ref.shape[1]
    hp = k_ref.shape[2] // HEAD_SLOT
    n_chunks = seq // kc
    nt_dims = (((1,), (1,)), ((), ()))
    bufs = (s_a, s_b)
    assert hp % 2 == 0

    def score_chunk(q, h, c):
        hs = slice(h * HEAD_SLOT, (h + 1) * HEAD_SLOT)
        ks = slice(c * kc, (c + 1) * kc)
        s = lax.dot_general(k_ref[0, ks, hs], q[0, :, hs], nt_dims,
                            preferred_element_type=F32)
        bufs[h % 2][ks, :] = s
        return _row_groups(s, jnp.max)

    def fold(cms):
        mx = cms[0]
        for cm in cms[1:]:
            mx = jnp.maximum(mx, cm)
        return mx

    @pl.when(pl.program_id(2) == 0)
    def _():
        mx_ref[...] = fold([score_chunk(q_ref, 0, c) for c in range(n_chunks)])

    mx = mx_ref[...]
    zero_bits = zero_ref[0:1, :]
    outs = []
    for h in range(hp):
        m = jnp.max(mx, axis=0, keepdims=True)
        ds = slice(h * V_SLOT, (h + 1) * V_SLOT)
        cms = []
        acc = None
        for c in range(n_chunks):
            ks = slice(c * kc, (c + 1) * kc)
            cms.append(score_chunk(q_ref, h + 1, c) if h + 1 < hp else score_chunk(qn_ref, 0, c))
            m_c = m
            if c >= SKEW:
                dep = lax.bitcast_convert_type(cms[c - SKEW][0:1, :], jnp.int32) & zero_bits
                m_c = m + lax.bitcast_convert_type(dep, F32)
            p = jnp.exp2(bufs[h % 2][ks, :] - m_c)
            o = jnp.dot(vt_ref[0, ds, ks], p.astype(BF16), preferred_element_type=F32)
            acc = o if acc is None else acc + o
        outs.append(acc[:V_HEAD_DIM] / acc[V_HEAD_DIM:V_HEAD_DIM + 1])
        mx = fold(cms)
    mx_ref[...] = mx
    o_ref[0] = jnp.concatenate(outs, axis=0).T.astype(BF16)


def _mix_kernel(u_ref, uprev_ref, unext_ref, attn_ref, x_ref, pw_ref, ps_ref, wo_ref, g_ref, b_ref,
                out_ref, ext_ref, *, seq, alpha):
    i = pl.program_id(1)
    n = pl.num_programs(1)
    tm = u_ref.shape[1]
    ext_ref[0:POOL_HALO] = jnp.where(i > 0, uprev_ref[0], 0.0)
    ext_ref[POOL_HALO:POOL_HALO + tm] = u_ref[0]
    ext_ref[POOL_HALO + tm:] = jnp.where(i < n - 1, unext_ref[0], 0.0)
    row = i * tm + lax.broadcasted_iota(jnp.int32, (tm, 1), 0)
    acc = jnp.zeros((tm, D_MODEL), F32)
    for g, w in enumerate(POOL_WINDOWS):
        cols = slice(g * POOL_GROUP_DIM, (g + 1) * POOL_GROUP_DIM)
        back = w // 2
        win = ext_ref[pl.ds(POOL_HALO - back, tm), cols]
        for d in range(1 - back, w - back):
            win = win + ext_ref[pl.ds(POOL_HALO + d, tm), cols]
        cnt = (jnp.minimum(row + (w - back), seq) - jnp.maximum(row - back, 0)).astype(F32)
        pg = win / cnt - ext_ref[pl.ds(POOL_HALO, tm), cols]
        og = jnp.dot(pg.astype(BF16), pw_ref[g], preferred_element_type=F32) * ps_ref[:, cols]
        acc = acc + jnp.dot(og.astype(BF16), wo_ref[cols, :], preferred_element_type=F32)
    acc = acc + jnp.dot(attn_ref[0], wo_ref[POOL_WIDTH:, :], preferred_element_type=F32)
    y = alpha * x_ref[0] + acc
    out_ref[0] = _layer_norm(y, g_ref[...], b_ref[...])


def _ffn_kernel(x_ref, wg_ref, wu_ref, wd_ref, g_ref, b_ref, out_ref, *, alpha):
    x = x_ref[...]
    xb = x.astype(BF16)
    gate = jnp.dot(xb, wg_ref[...], preferred_element_type=F32)
    up = jnp.dot(xb, wu_ref[...], preferred_element_type=F32)
    hid = gate * (1.0 / (1.0 + jnp.exp(-gate))) * up
    ffn = jnp.dot(hid.astype(BF16), wd_ref[...], preferred_element_type=F32)
    y = alpha * x + ffn
    out_ref[...] = _layer_norm(y, g_ref[...], b_ref[...])


def _full(shape):
    return pl.BlockSpec(shape, lambda *_: (0,) * len(shape))


def _params(sem):
    return pltpu.CompilerParams(dimension_semantics=sem, vmem_limit_bytes=VMEM_LIMIT)


def _pad_heads(w, dim):
    r = w.shape[0]
    w = w.reshape(r, N_HEADS, dim)
    return jnp.pad(w, ((0, 0), (0, 0), (0, HEAD_SLOT - dim))).reshape(r, N_HEADS * HEAD_SLOT)


def _rope_slot(w):
    return jnp.concatenate([w, w[..., :ROPE_HALF], jnp.zeros_like(w[..., :ROPE_HALF])], axis=-1)


def _layer(x, cos_t, sin_t, w_in, pool_w, pool_scale, q_norm_g, w_q_up, kv_norm_g, w_k_up, w_v_up,
           w_o, ln1_g, ln1_b, w_gate, w_up, w_down, ln2_g, ln2_b, alpha):
    bsz, seq, _ = x.shape
    tm = 512
    tq = 512
    o_kr = POOL_WIDTH + Q_LORA_RANK + KV_LORA_RANK

    w_in_ext = jnp.concatenate(
        [w_in[:, :o_kr], jnp.zeros((D_MODEL, ROPE_LO), w_in.dtype), _rope_slot(w_in[:, o_kr:])],
        axis=1).astype(BF16)
    wq3 = w_q_up.reshape(Q_LORA_RANK, N_HEADS, QK_HEAD_DIM)
    wq = jnp.concatenate([wq3[..., :QK_NOPE_DIM], _rope_slot(wq3[..., QK_NOPE_DIM:])],
                         axis=-1).reshape(Q_LORA_RANK, N_HEADS * HEAD_SLOT).astype(BF16)
    wk = _pad_heads(w_k_up, QK_NOPE_DIM).astype(BF16)
    wv = jnp.pad(w_v_up.T.reshape(N_HEADS, V_HEAD_DIM, KV_LORA_RANK),
                 ((0, 0), (0, V_SLOT - V_HEAD_DIM), (0, 0))).reshape(VT_ROWS, KV_LORA_RANK).astype(BF16)
    ones_col = (jnp.arange(VT_ROWS) % V_SLOT == V_HEAD_DIM).astype(F32)[:, None]
    q_scale = (QK_HEAD_DIM ** -0.5) * math.log2(math.e)

    tok = lambda w: pl.BlockSpec((1, tm, w), lambda b, i: (b, i, 0))
    rope = lambda: pl.BlockSpec((ROPE_HALF, tm), lambda b, i: (0, b * (seq // tm) + i))
    u, q, k, v = pl.pallas_call(
        functools.partial(_proj_kernel, q_scale=q_scale),
        grid=(bsz, seq // tm),
        in_specs=[tok(D_MODEL), rope(), rope(), _full((D_MODEL, H_WIDTH)),
                  _full((1, Q_LORA_RANK)), _full((Q_LORA_RANK, N_HEADS * HEAD_SLOT)),
                  _full((1, KV_LORA_RANK)), _full((KV_LORA_RANK, N_HEADS * HEAD_SLOT)),
                  _full((VT_ROWS, KV_LORA_RANK)), _full((VT_ROWS, 1))],
        out_specs=[tok(POOL_WIDTH), tok(N_HEADS * HEAD_SLOT), tok(N_HEADS * HEAD_SLOT),
                   pl.BlockSpec((1, VT_ROWS, tm), lambda b, i: (b, 0, i))],
        out_shape=[jax.ShapeDtypeStruct((bsz, seq, POOL_WIDTH), F32),
                   jax.ShapeDtypeStruct((bsz, seq, N_HEADS * HEAD_SLOT), BF16),
                   jax.ShapeDtypeStruct((bsz, seq, N_HEADS * HEAD_SLOT), BF16),
                   jax.ShapeDtypeStruct((bsz, VT_ROWS, seq), BF16)],
        compiler_params=_params(("parallel", "parallel")),
        name="proj",
    )(x, cos_t, sin_t, w_in_ext, q_norm_g[None, :], wq, kv_norm_g[None, :], wk, wv, ones_col)

    hp = 4
    attn = pl.pallas_call(
        functools.partial(_attn_kernel, kc=256),
        grid=(bsz, N_HEADS // hp, seq // tq),
        in_specs=[pl.BlockSpec((1, tq, hp * HEAD_SLOT), lambda b, g, i: (b, i, g)),
                  pl.BlockSpec((1, tq, hp * HEAD_SLOT),
                               lambda b, g, i: (b, jnp.minimum(i + 1, seq // tq - 1), g)),
                  pl.BlockSpec((1, seq, hp * HEAD_SLOT), lambda b, g, i: (b, 0, g)),
                  pl.BlockSpec((1, hp * V_SLOT, seq), lambda b, g, i: (b, g, 0)),
                  _full((SUBLANES, tq))],
        out_specs=pl.BlockSpec((1, tq, hp * V_HEAD_DIM), lambda b, g, i: (b, i, g)),
        out_shape=jax.ShapeDtypeStruct((bsz, seq, ATTN_WIDTH), BF16),
        scratch_shapes=[pltpu.VMEM((seq, tq), F32), pltpu.VMEM((seq, tq), F32),
                        pltpu.VMEM((SUBLANES, tq), F32)],
        compiler_params=_params(("parallel", "parallel", "arbitrary")),
        name="attn",
    )(q, q, k, v, jnp.zeros((SUBLANES, tq), jnp.int32))

    halo_blocks = tm // POOL_HALO
    last_halo = seq // POOL_HALO - 1
    x1 = pl.pallas_call(
        functools.partial(_mix_kernel, seq=seq, alpha=alpha),
        grid=(bsz, seq // tm),
        in_specs=[tok(POOL_WIDTH),
                  pl.BlockSpec((1, POOL_HALO, POOL_WIDTH),
                               lambda b, i: (b, jnp.maximum(i * halo_blocks - 1, 0), 0)),
                  pl.BlockSpec((1, POOL_HALO, POOL_WIDTH),
                               lambda b, i: (b, jnp.minimum((i + 1) * halo_blocks, last_halo), 0)),
                  tok(ATTN_WIDTH), tok(D_MODEL),
                  _full((len(POOL_WINDOWS), POOL_GROUP_DIM, POOL_GROUP_DIM)), _full((1, POOL_WIDTH)),
                  _full((D_MODEL, D_MODEL)), _full((1, D_MODEL)), _full((1, D_MODEL))],
        out_specs=tok(D_MODEL),
        out_shape=jax.ShapeDtypeStruct((bsz, seq, D_MODEL), F32),
        scratch_shapes=[pltpu.VMEM((tm + 2 * POOL_HALO, POOL_WIDTH), F32)],
        compiler_params=_params(("parallel", "arbitrary")),
        name="mix",
    )(u, u, u, attn, x, pool_w.astype(BF16), pool_scale[None, :], w_o.astype(BF16),
      ln1_g[None, :], ln1_b[None, :])

    tf = 256
    rows = bsz * seq
    wspec = lambda shape: pl.BlockSpec(shape, lambda i: (0, 0), pipeline_mode=pl.Buffered(1))
    x2 = pl.pallas_call(
        functools.partial(_ffn_kernel, alpha=alpha),
        grid=(rows // tf,),
        in_specs=[pl.BlockSpec((tf, D_MODEL), lambda i: (i, 0)),
                  wspec((D_MODEL, FFN_HIDDEN)), wspec((D_MODEL, FFN_HIDDEN)), wspec((FFN_HIDDEN, D_MODEL)),
                  _full((1, D_MODEL)), _full((1, D_MODEL))],
        out_specs=pl.BlockSpec((tf, D_MODEL), lambda i: (i, 0)),
        out_shape=jax.ShapeDtypeStruct((rows, D_MODEL), F32),
        compiler_params=_params(("parallel",)),
        name="ffn",
    )(x1.reshape(rows, D_MODEL), w_gate.astype(BF16), w_up.astype(BF16), w_down.astype(BF16),
      ln2_g[None, :], ln2_b[None, :])
    return x2.reshape(bsz, seq, D_MODEL)


def kernel(x, positions, w_in, pool_w, pool_scale, q_norm_g, w_q_up, kv_norm_g, w_k_up, w_v_up, w_o, ln1_g, ln1_b, w_gate, w_up, w_down, ln2_g, ln2_b):
    depth = w_in.shape[0]
    alpha = (2.0 * depth) ** 0.25
    inv_freq = 1.0 / (ROPE_THETA ** (jnp.arange(0, QK_ROPE_DIM, 2, dtype=F32) / QK_ROPE_DIM))
    tokens = positions.size
    plane = (ROPE_HALF, tokens // LANES, LANES)
    cos_t, sin_t = pl.pallas_call(
        _rope_table_kernel,
        in_specs=[pl.BlockSpec(memory_space=pltpu.SMEM), _full(plane[1:])],
        out_specs=[_full(plane), _full(plane)],
        out_shape=[jax.ShapeDtypeStruct(plane, F32)] * 2,
        grid=(1,),
        name="rope_table",
    )(inv_freq, positions.reshape(plane[1:]))
    cos_t = cos_t.reshape(ROPE_HALF, tokens)
    sin_t = sin_t.reshape(ROPE_HALF, tokens)
    for l in range(depth):
        x = _layer(x, cos_t, sin_t, w_in[l], pool_w[l], pool_scale[l], q_norm_g[l], w_q_up[l],
                   kv_norm_g[l], w_k_up[l], w_v_up[l], w_o[l], ln1_g[l], ln1_b[l],
                   w_gate[l], w_up[l], w_down[l], ln2_g[l], ln2_b[l], alpha)
    return x
```

```python
import functools
import math

import jax
import jax.numpy as jnp
from jax import lax
from jax.experimental import pallas as pl
from jax.experimental.pallas import tpu as pltpu

D_MODEL = 1024
POOL_WIDTH = 512
POOL_WINDOWS = (2, 4, 8, 16)
POOL_GROUP_DIM = 128
N_HEADS = 8
QK_NOPE_DIM = 64
QK_ROPE_DIM = 32
V_HEAD_DIM = 64
QK_HEAD_DIM = QK_NOPE_DIM + QK_ROPE_DIM
ATTN_WIDTH = N_HEADS * V_HEAD_DIM
Q_LORA_RANK = 384
KV_LORA_RANK = 256
ROPE_THETA = 10000.0
FFN_HIDDEN = 2816
LN_EPS = 1e-5
RMS_EPS = 1e-6

LANES = 128
SUBLANES = 8
BF16_ROWS = 2 * SUBLANES
V_SLOT = V_HEAD_DIM + BF16_ROWS
VT_ROWS = N_HEADS * V_SLOT
HEAD_SLOT = LANES
ROPE_LO = QK_NOPE_DIM
ROPE_HALF = QK_ROPE_DIM // 2
H_WIDTH = POOL_WIDTH + Q_LORA_RANK + KV_LORA_RANK + LANES
POOL_HALO = 8
VMEM_LIMIT = 56 * 1024 * 1024

BF16 = jnp.bfloat16
F32 = jnp.float32


def _rope(t, cos, sin_signed):
    partner = pltpu.roll(t, LANES - ROPE_HALF, 1)
    return t * cos + partner * sin_signed


def _rms(t, g):
    y = t * lax.rsqrt(jnp.mean(t * t, axis=-1, keepdims=True) + RMS_EPS)
    return y * g


def _layer_norm(y, g, b):
    mu = jnp.mean(y, axis=-1, keepdims=True)
    d = y - mu
    var = jnp.mean(d * d, axis=-1, keepdims=True)
    return d * lax.rsqrt(var + LN_EPS) * g + b


def _rope_table_kernel(invf_ref, pos_ref, cos_ref, sin_ref):
    pos = pos_ref[...].astype(F32)
    for f in range(ROPE_HALF):
        ang = pos * invf_ref[f]
        cos_ref[f] = jnp.cos(ang)
        sin_ref[f] = jnp.sin(ang)


def _rope_lanes(t, outside):
    tm = t.shape[1]
    full = jnp.concatenate([jnp.full((ROPE_LO, tm), outside, F32), t, t,
                            jnp.zeros((LANES - ROPE_LO - QK_ROPE_DIM, tm), F32)], axis=0)
    return full.T


def _proj_kernel(x_ref, cos_ref, sin_ref, w_in_ref, qg_ref, wq_ref, kvg_ref, wk_ref, wv_ref, ones_ref,
                 u_ref, q_ref, k_ref, v_ref, *, q_scale):
    x = x_ref[0].astype(BF16)
    h = jnp.dot(x, w_in_ref[...], preferred_element_type=F32)
    u_ref[0] = h[:, :POOL_WIDTH]
    o_kv = POOL_WIDTH + Q_LORA_RANK
    o_kr = o_kv + KV_LORA_RANK
    cq = _rms(h[:, POOL_WIDTH:o_kv], qg_ref[...])
    ckv = _rms(h[:, o_kv:o_kr], kvg_ref[...])
    kr = h[:, o_kr:]

    cos = _rope_lanes(cos_ref[...], 1.0)
    sin = _rope_lanes(sin_ref[...], 0.0)
    first_half = lax.broadcasted_iota(jnp.int32, cos.shape, 1) < ROPE_LO + ROPE_HALF
    sin = jnp.where(first_half, -sin, sin)
    cos_q = cos * q_scale
    sin_q = sin * q_scale

    q = jnp.dot(cq.astype(BF16), wq_ref[...], preferred_element_type=F32)
    ckv_b = ckv.astype(BF16)
    k_nope = jnp.dot(ckv_b, wk_ref[...], preferred_element_type=F32)
    vt = lax.dot_general(wv_ref[...], ckv_b, (((1,), (1,)), ((), ())), preferred_element_type=F32)
    v_ref[0] = (vt + ones_ref[...]).astype(BF16)

    k_rope = _rope(kr, cos, sin)
    for hd in range(N_HEADS):
        sl = slice(hd * HEAD_SLOT, (hd + 1) * HEAD_SLOT)
        q_ref[0, :, sl] = _rope(q[:, sl], cos_q, sin_q).astype(BF16)
        k_ref[0, :, sl] = (k_nope[:, sl] + k_rope).astype(BF16)


def _row_groups(t, op):
    return op(t.reshape(t.shape[0] // SUBLANES, SUBLANES, t.shape[1]), axis=0)


SKEW = 4


def _attn_kernel(q_ref, qn_ref, k_ref, vt_ref, zero_ref, o_ref, s_ref, mx_ref, *, kc):
    seq = k_ref.shape[1]
    hp = k_ref.shape[2] // HEAD_SLOT
    n_chunks = seq // kc
    nt_dims = (((1,), (1,)), ((), ()))
    assert hp % 2 == 0

    def score_chunk(q, h, c):
        hs = slice(h * HEAD_SLOT, (h + 1) * HEAD_SLOT)
        ks = slice(c * kc, (c + 1) * kc)
        s = lax.dot_general(k_ref[0, ks, hs], q[0, :, hs], nt_dims,
                            preferred_element_type=F32)
        s_ref[h % 2, ks, :] = s
        return _row_groups(s, jnp.max)

    def fold(cms):
        mx = cms[0]
        for cm in cms[1:]:
            mx = jnp.maximum(mx, cm)
        return mx

    @pl.when(pl.program_id(2) == 0)
    def _():
        mx_ref[...] = fold([score_chunk(q_ref, 0, c) for c in range(n_chunks)])

    mx = mx_ref[...]
    zero_bits = zero_ref[0:1, :]
    outs = []
    for h in range(hp):
        m = jnp.max(mx, axis=0, keepdims=True)
        ds = slice(h * V_SLOT, (h + 1) * V_SLOT)
        cms = []
        acc = None
        for c in range(n_chunks):
            ks = slice(c * kc, (c + 1) * kc)
            cms.append(score_chunk(q_ref, h + 1, c) if h + 1 < hp else score_chunk(qn_ref, 0, c))
            m_c = m
            if c >= SKEW:
                dep = lax.bitcast_convert_type(cms[c - SKEW][0:1, :], jnp.int32) & zero_bits
                m_c = m + lax.bitcast_convert_type(dep, F32)
            p = jnp.exp2(s_ref[h % 2, ks, :] - m_c)
            o = jnp.dot(vt_ref[0, ds, ks], p.astype(BF16), preferred_element_type=F32)
            acc = o if acc is None else acc + o
        outs.append(acc[:V_HEAD_DIM] / acc[V_HEAD_DIM:V_HEAD_DIM + 1])
        mx = fold(cms)
    mx_ref[...] = mx
    o_ref[0] = jnp.concatenate(outs, axis=0).T.astype(BF16)


def _mix_kernel(u_ref, uprev_ref, unext_ref, attn_ref, x_ref, pw_ref, ps_ref, wo_ref, g_ref, b_ref,
                out_ref, ext_ref, *, seq, alpha):
    i = pl.program_id(1)
    n = pl.num_programs(1)
    tm = u_ref.shape[1]
    ext_ref[0:POOL_HALO] = jnp.where(i > 0, uprev_ref[0], 0.0)
    ext_ref[POOL_HALO:POOL_HALO + tm] = u_ref[0]
    ext_ref[POOL_HALO + tm:] = jnp.where(i < n - 1, unext_ref[0], 0.0)
    row = i * tm + lax.broadcasted_iota(jnp.int32, (tm, 1), 0)
    acc = jnp.zeros((tm, D_MODEL), F32)
    for g, w in enumerate(POOL_WINDOWS):
        cols = slice(g * POOL_GROUP_DIM, (g + 1) * POOL_GROUP_DIM)
        back = w // 2
        win = ext_ref[pl.ds(POOL_HALO - back, tm), cols]
        for d in range(1 - back, w - back):
            win = win + ext_ref[pl.ds(POOL_HALO + d, tm), cols]
        cnt = (jnp.minimum(row + (w - back), seq) - jnp.maximum(row - back, 0)).astype(F32)
        pg = win / cnt - ext_ref[pl.ds(POOL_HALO, tm), cols]
        og = jnp.dot(pg.astype(BF16), pw_ref[g], preferred_element_type=F32) * ps_ref[:, cols]
        acc = acc + jnp.dot(og.astype(BF16), wo_ref[cols, :], preferred_element_type=F32)
    acc = acc + jnp.dot(attn_ref[0], wo_ref[POOL_WIDTH:, :], preferred_element_type=F32)
    y = alpha * x_ref[0] + acc
    out_ref[0] = _layer_norm(y, g_ref[...], b_ref[...])


def _ffn_kernel(x_ref, wg_ref, wu_ref, wd_ref, g_ref, b_ref, out_ref, *, alpha):
    x = x_ref[...]
    xb = x.astype(BF16)
    gate = jnp.dot(xb, wg_ref[...], preferred_element_type=F32)
    up = jnp.dot(xb, wu_ref[...], preferred_element_type=F32)
    hid = gate * (1.0 / (1.0 + jnp.exp(-gate))) * up
    ffn = jnp.dot(hid.astype(BF16), wd_ref[...], preferred_element_type=F32)
    y = alpha * x + ffn
    out_ref[...] = _layer_norm(y, g_ref[...], b_ref[...])


def _full(shape):
    return pl.BlockSpec(shape, lambda *_: (0,) * len(shape))


def _params(sem):
    return pltpu.CompilerParams(dimension_semantics=sem, vmem_limit_bytes=VMEM_LIMIT)


def _pad_heads(w, dim):
    r = w.shape[0]
    w = w.reshape(r, N_HEADS, dim)
    return jnp.pad(w, ((0, 0), (0, 0), (0, HEAD_SLOT - dim))).reshape(r, N_HEADS * HEAD_SLOT)


def _rope_slot(w):
    return jnp.concatenate([w, w[..., :ROPE_HALF], jnp.zeros_like(w[..., :ROPE_HALF])], axis=-1)


def _layer(x, cos_t, sin_t, w_in, pool_w, pool_scale, q_norm_g, w_q_up, kv_norm_g, w_k_up, w_v_up,
           w_o, ln1_g, ln1_b, w_gate, w_up, w_down, ln2_g, ln2_b, alpha):
    bsz, seq, _ = x.shape
    tm = 512
    tq = 512
    o_kr = POOL_WIDTH + Q_LORA_RANK + KV_LORA_RANK

    w_in_ext = jnp.concatenate(
        [w_in[:, :o_kr], jnp.zeros((D_MODEL, ROPE_LO), w_in.dtype), _rope_slot(w_in[:, o_kr:])],
        axis=1).astype(BF16)
    wq3 = w_q_up.reshape(Q_LORA_RANK, N_HEADS, QK_HEAD_DIM)
    wq = jnp.concatenate([wq3[..., :QK_NOPE_DIM], _rope_slot(wq3[..., QK_NOPE_DIM:])],
                         axis=-1).reshape(Q_LORA_RANK, N_HEADS * HEAD_SLOT).astype(BF16)
    wk = _pad_heads(w_k_up, QK_NOPE_DIM).astype(BF16)
    wv = jnp.pad(w_v_up.T.reshape(N_HEADS, V_HEAD_DIM, KV_LORA_RANK),
                 ((0, 0), (0, V_SLOT - V_HEAD_DIM), (0, 0))).reshape(VT_ROWS, KV_LORA_RANK).astype(BF16)
    ones_col = (jnp.arange(VT_ROWS) % V_SLOT == V_HEAD_DIM).astype(F32)[:, None]
    q_scale = (QK_HEAD_DIM ** -0.5) * math.log2(math.e)

    tok = lambda w: pl.BlockSpec((1, tm, w), lambda b, i: (b, i, 0))
    rope = lambda: pl.BlockSpec((ROPE_HALF, tm), lambda b, i: (0, b * (seq // tm) + i))
    u, q, k, v = pl.pallas_call(
        functools.partial(_proj_kernel, q_scale=q_scale),
        grid=(bsz, seq // tm),
        in_specs=[tok(D_MODEL), rope(), rope(), _full((D_MODEL, H_WIDTH)),
                  _full((1, Q_LORA_RANK)), _full((Q_LORA_RANK, N_HEADS * HEAD_SLOT)),
                  _full((1, KV_LORA_RANK)), _full((KV_LORA_RANK, N_HEADS * HEAD_SLOT)),
                  _full((VT_ROWS, KV_LORA_RANK)), _full((VT_ROWS, 1))],
        out_specs=[tok(POOL_WIDTH), tok(N_HEADS * HEAD_SLOT), tok(N_HEADS * HEAD_SLOT),
                   pl.BlockSpec((1, VT_ROWS, tm), lambda b, i: (b, 0, i))],
        out_shape=[jax.ShapeDtypeStruct((bsz, seq, POOL_WIDTH), F32),
                   jax.ShapeDtypeStruct((bsz, seq, N_HEADS * HEAD_SLOT), BF16),
                   jax.ShapeDtypeStruct((bsz, seq, N_HEADS * HEAD_SLOT), BF16),
                   jax.ShapeDtypeStruct((bsz, VT_ROWS, seq), BF16)],
        compiler_params=_params(("parallel", "parallel")),
        name="proj",
    )(x, cos_t, sin_t, w_in_ext, q_norm_g[None, :], wq, kv_norm_g[None, :], wk, wv, ones_col)

    hp = 4
    attn = pl.pallas_call(
        functools.partial(_attn_kernel, kc=256),
        grid=(bsz, N_HEADS // hp, seq // tq),
        in_specs=[pl.BlockSpec((1, tq, hp * HEAD_SLOT), lambda b, g, i: (b, i, g)),
                  pl.BlockSpec((1, tq, hp * HEAD_SLOT),
                               lambda b, g, i: (b, jnp.minimum(i + 1, seq // tq - 1), g)),
                  pl.BlockSpec((1, seq, hp * HEAD_SLOT), lambda b, g, i: (b, 0, g)),
                  pl.BlockSpec((1, hp * V_SLOT, seq), lambda b, g, i: (b, g, 0)),
                  _full((SUBLANES, tq))],
        out_specs=pl.BlockSpec((1, tq, hp * V_HEAD_DIM), lambda b, g, i: (b, i, g)),
        out_shape=jax.ShapeDtypeStruct((bsz, seq, ATTN_WIDTH), BF16),
        scratch_shapes=[pltpu.VMEM((2, seq, tq), F32),
                        pltpu.VMEM((SUBLANES, tq), F32)],
        compiler_params=_params(("parallel", "parallel", "arbitrary")),
        name="attn",
    )(q, q, k, v, jnp.zeros((SUBLANES, tq), jnp.int32))

    halo_blocks = tm // POOL_HALO
    last_halo = seq // POOL_HALO - 1
    x1 = pl.pallas_call(
        functools.partial(_mix_kernel, seq=seq, alpha=alpha),
        grid=(bsz, seq // tm),
        in_specs=[tok(POOL_WIDTH),
                  pl.BlockSpec((1, POOL_HALO, POOL_WIDTH),
                               lambda b, i: (b, jnp.maximum(i * halo_blocks - 1, 0), 0)),
                  pl.BlockSpec((1, POOL_HALO, POOL_WIDTH),
                               lambda b, i: (b, jnp.minimum((i + 1) * halo_blocks, last_halo), 0)),
                  tok(ATTN_WIDTH), tok(D_MODEL),
                  _full((len(POOL_WINDOWS), POOL_GROUP_DIM, POOL_GROUP_DIM)), _full((1, POOL_WIDTH)),
                  _full((D_MODEL, D_MODEL)), _full((1, D_MODEL)), _full((1, D_MODEL))],
        out_specs=tok(D_MODEL),
        out_shape=jax.ShapeDtypeStruct((bsz, seq, D_MODEL), F32),
        scratch_shapes=[pltpu.VMEM((tm + 2 * POOL_HALO, POOL_WIDTH), F32)],
        compiler_params=_params(("parallel", "arbitrary")),
        name="mix",
    )(u, u, u, attn, x, pool_w.astype(BF16), pool_scale[None, :], w_o.astype(BF16),
      ln1_g[None, :], ln1_b[None, :])

    tf = 256
    rows = bsz * seq
    wspec = lambda shape: pl.BlockSpec(shape, lambda i: (0, 0), pipeline_mode=pl.Buffered(1))
    x2 = pl.pallas_call(
        functools.partial(_ffn_kernel, alpha=alpha),
        grid=(rows // tf,),
        in_specs=[pl.BlockSpec((tf, D_MODEL), lambda i: (i, 0)),
                  wspec((D_MODEL, FFN_HIDDEN)), wspec((D_MODEL, FFN_HIDDEN)), wspec((FFN_HIDDEN, D_MODEL)),
                  _full((1, D_MODEL)), _full((1, D_MODEL))],
        out_specs=pl.BlockSpec((tf, D_MODEL), lambda i: (i, 0)),
        out_shape=jax.ShapeDtypeStruct((rows, D_MODEL), F32),
        compiler_params=_params(("parallel",)),
        name="ffn",
    )(x1.reshape(rows, D_MODEL), w_gate.astype(BF16), w_up.astype(BF16), w_down.astype(BF16),
      ln2_g[None, :], ln2_b[None, :])
    return x2.reshape(bsz, seq, D_MODEL)


def kernel(x, positions, w_in, pool_w, pool_scale, q_norm_g, w_q_up, kv_norm_g, w_k_up, w_v_up, w_o, ln1_g, ln1_b, w_gate, w_up, w_down, ln2_g, ln2_b):
    depth = w_in.shape[0]
    alpha = (2.0 * depth) ** 0.25
    inv_freq = 1.0 / (ROPE_THETA ** (jnp.arange(0, QK_ROPE_DIM, 2, dtype=F32) / QK_ROPE_DIM))
    tokens = positions.size
    plane = (ROPE_HALF, tokens // LANES, LANES)
    cos_t, sin_t = pl.pallas_call(
        _rope_table_kernel,
        in_specs=[pl.BlockSpec(memory_space=pltpu.SMEM), _full(plane[1:])],
        out_specs=[_full(plane), _full(plane)],
        out_shape=[jax.ShapeDtypeStruct(plane, F32)] * 2,
        grid=(1,),
        name="rope_table",
    )(inv_freq, positions.reshape(plane[1:]))
    cos_t = cos_t.reshape(ROPE_HALF, tokens)
    sin_t = sin_t.reshape(ROPE_HALF, tokens)
    for l in range(depth):
        x = _layer(x, cos_t, sin_t, w_in[l], pool_w[l], pool_scale[l], q_norm_g[l], w_q_up[l],
                   kv_norm_g[l], w_k_up[l], w_v_up[l], w_o[l], ln1_g[l], ln1_b[l],
                   w_gate[l], w_up[l], w_down[l], ln2_g[l], ln2_b[l], alpha)
    return x
```

```python
import functools
import math

import jax
import jax.numpy as jnp
from jax import lax
from jax.experimental import pallas as pl
from jax.experimental.pallas import tpu as pltpu

D_MODEL = 1024
POOL_WIDTH = 512
POOL_WINDOWS = (2, 4, 8, 16)
POOL_GROUP_DIM = 128
N_HEADS = 8
QK_NOPE_DIM = 64
QK_ROPE_DIM = 32
V_HEAD_DIM = 64
QK_HEAD_DIM = QK_NOPE_DIM + QK_ROPE_DIM
ATTN_WIDTH = N_HEADS * V_HEAD_DIM
Q_LORA_RANK = 384
KV_LORA_RANK = 256
ROPE_THETA = 10000.0
FFN_HIDDEN = 2816
LN_EPS = 1e-5
RMS_EPS = 1e-6

LANES = 128
SUBLANES = 8
BF16_ROWS = 2 * SUBLANES
V_SLOT = V_HEAD_DIM + BF16_ROWS
VT_ROWS = N_HEADS * V_SLOT
HEAD_SLOT = LANES
ROPE_LO = QK_NOPE_DIM
ROPE_HALF = QK_ROPE_DIM // 2
H_WIDTH = POOL_WIDTH + Q_LORA_RANK + KV_LORA_RANK + LANES
POOL_HALO = 8
VMEM_LIMIT = 56 * 1024 * 1024

BF16 = jnp.bfloat16
F32 = jnp.float32


def _rope(t, cos, sin_signed):
    partner = pltpu.roll(t, LANES - ROPE_HALF, 1)
    return t * cos + partner * sin_signed


def _rms(t, g):
    y = t * lax.rsqrt(jnp.mean(t * t, axis=-1, keepdims=True) + RMS_EPS)
    return y * g


def _layer_norm(y, g, b):
    mu = jnp.mean(y, axis=-1, keepdims=True)
    d = y - mu
    var = jnp.mean(d * d, axis=-1, keepdims=True)
    return d * lax.rsqrt(var + LN_EPS) * g + b


def _rope_table_kernel(invf_ref, pos_ref, cos_ref, sin_ref):
    pos = pos_ref[...].astype(F32)
    for f in range(ROPE_HALF):
        ang = pos * invf_ref[f]
        cos_ref[f] = jnp.cos(ang)
        sin_ref[f] = jnp.sin(ang)


def _rope_lanes(t, outside):
    tm = t.shape[1]
    full = jnp.concatenate([jnp.full((ROPE_LO, tm), outside, F32), t, t,
                            jnp.zeros((LANES - ROPE_LO - QK_ROPE_DIM, tm), F32)], axis=0)
    return full.T


def _proj_kernel(x_ref, cos_ref, sin_ref, w_in_ref, qg_ref, wq_ref, kvg_ref, wk_ref, wv_ref, ones_ref,
                 u_ref, q_ref, k_ref, v_ref, *, q_scale):
    x = x_ref[0].astype(BF16)
    h = jnp.dot(x, w_in_ref[...], preferred_element_type=F32)
    u_ref[0] = h[:, :POOL_WIDTH]
    o_kv = POOL_WIDTH + Q_LORA_RANK
    o_kr = o_kv + KV_LORA_RANK
    cq = _rms(h[:, POOL_WIDTH:o_kv], qg_ref[...])
    ckv = _rms(h[:, o_kv:o_kr], kvg_ref[...])
    kr = h[:, o_kr:]

    cos_t = cos_ref[...]
    sin_t = sin_ref[...]
    cos = _rope_lanes(cos_t, 1.0)
    sin = _rope_lanes(sin_t, 0.0)
    first_half = lax.broadcasted_iota(jnp.int32, cos.shape, 1) < ROPE_LO + ROPE_HALF
    sin = jnp.where(first_half, -sin, sin)

    cq_b = cq.astype(BF16)
    ckv_b = ckv.astype(BF16)
    nt_dims = (((1,), (1,)), ((), ()))
    qt = lax.dot_general(wq_ref[...], cq_b, nt_dims, preferred_element_type=F32)
    k_nope = jnp.dot(ckv_b, wk_ref[...], preferred_element_type=F32)
    vt = lax.dot_general(wv_ref[...], ckv_b, nt_dims, preferred_element_type=F32)
    v_ref[0] = (vt + ones_ref[...]).astype(BF16)

    cos_q = cos_t * q_scale
    sin_q = sin_t * q_scale
    k_rope = _rope(kr, cos, sin)
    for hd in range(N_HEADS):
        sl = slice(hd * HEAD_SLOT, (hd + 1) * HEAD_SLOT)
        k_ref[0, :, sl] = (k_nope[:, sl] + k_rope).astype(BF16)
        base = hd * HEAD_SLOT
        r1 = qt[base + ROPE_LO:base + ROPE_LO + ROPE_HALF]
        r2 = qt[base + ROPE_LO + ROPE_HALF:base + ROPE_LO + QK_ROPE_DIM]
        q_ref[0, base:base + ROPE_LO, :] = (qt[base:base + ROPE_LO] * q_scale).astype(BF16)
        q_ref[0, base + ROPE_LO:base + HEAD_SLOT, :] = jnp.concatenate(
            [r1 * cos_q - r2 * sin_q, r2 * cos_q + r1 * sin_q,
             jnp.zeros((HEAD_SLOT - QK_HEAD_DIM, qt.shape[1]), F32)], axis=0).astype(BF16)


def _row_groups(t, op):
    return op(t.reshape(t.shape[0] // SUBLANES, SUBLANES, t.shape[1]), axis=0)


SKEW = 4


def _attn_kernel(q_ref, qn_ref, k_ref, vt_ref, zero_ref, o_ref, s_ref, mx_ref, *, kc):
    seq = k_ref.shape[1]
    hp = k_ref.shape[2] // HEAD_SLOT
    n_chunks = seq // kc
    assert hp % 2 == 0

    def score_chunk(q, h, c):
        hs = slice(h * HEAD_SLOT, (h + 1) * HEAD_SLOT)
        ks = slice(c * kc, (c + 1) * kc)
        s = jnp.dot(k_ref[0, ks, hs], q[0, hs, :], preferred_element_type=F32)
        s_ref[h % 2, ks, :] = s
        return _row_groups(s, jnp.max)

    def fold(cms):
        mx = cms[0]
        for cm in cms[1:]:
            mx = jnp.maximum(mx, cm)
        return mx

    @pl.when(pl.program_id(2) == 0)
    def _():
        mx_ref[...] = fold([score_chunk(q_ref, 0, c) for c in range(n_chunks)])

    mx = mx_ref[...]
    zero_bits = zero_ref[0:1, :]
    outs = []
    for h in range(hp):
        m = jnp.max(mx, axis=0, keepdims=True)
        ds = slice(h * V_SLOT, (h + 1) * V_SLOT)
        cms = []
        acc = None
        for c in range(n_chunks):
            ks = slice(c * kc, (c + 1) * kc)
            cms.append(score_chunk(q_ref, h + 1, c) if h + 1 < hp else score_chunk(qn_ref, 0, c))
            m_c = m
            if c >= SKEW:
                dep = lax.bitcast_convert_type(cms[c - SKEW][0:1, :], jnp.int32) & zero_bits
                m_c = m + lax.bitcast_convert_type(dep, F32)
            p = jnp.exp2(s_ref[h % 2, ks, :] - m_c)
            o = jnp.dot(vt_ref[0, ds, ks], p.astype(BF16), preferred_element_type=F32)
            acc = o if acc is None else acc + o
        outs.append(acc[:V_HEAD_DIM] / acc[V_HEAD_DIM:V_HEAD_DIM + 1])
        mx = fold(cms)
    mx_ref[...] = mx
    o_ref[0] = jnp.concatenate(outs, axis=0).T.astype(BF16)


def _mix_kernel(u_ref, uprev_ref, unext_ref, attn_ref, x_ref, pw_ref, ps_ref, wo_ref, g_ref, b_ref,
                out_ref, ext_ref, *, seq, alpha):
    i = pl.program_id(1)
    n = pl.num_programs(1)
    tm = u_ref.shape[1]
    ext_ref[0:POOL_HALO] = jnp.where(i > 0, uprev_ref[0], 0.0)
    ext_ref[POOL_HALO:POOL_HALO + tm] = u_ref[0]
    ext_ref[POOL_HALO + tm:] = jnp.where(i < n - 1, unext_ref[0], 0.0)
    row = i * tm + lax.broadcasted_iota(jnp.int32, (tm, 1), 0)
    acc = jnp.zeros((tm, D_MODEL), F32)
    for g, w in enumerate(POOL_WINDOWS):
        cols = slice(g * POOL_GROUP_DIM, (g + 1) * POOL_GROUP_DIM)
        back = w // 2
        win = ext_ref[pl.ds(POOL_HALO - back, tm), cols]
        for d in range(1 - back, w - back):
            win = win + ext_ref[pl.ds(POOL_HALO + d, tm), cols]
        cnt = (jnp.minimum(row + (w - back), seq) - jnp.maximum(row - back, 0)).astype(F32)
        pg = win / cnt - ext_ref[pl.ds(POOL_HALO, tm), cols]
        og = jnp.dot(pg.astype(BF16), pw_ref[g], preferred_element_type=F32) * ps_ref[:, cols]
        acc = acc + jnp.dot(og.astype(BF16), wo_ref[cols, :], preferred_element_type=F32)
    acc = acc + jnp.dot(attn_ref[0], wo_ref[POOL_WIDTH:, :], preferred_element_type=F32)
    y = alpha * x_ref[0] + acc
    out_ref[0] = _layer_norm(y, g_ref[...], b_ref[...])


def _ffn_kernel(x_ref, wg_ref, wu_ref, wd_ref, g_ref, b_ref, out_ref, *, alpha):
    x = x_ref[...]
    xb = x.astype(BF16)
    gate = jnp.dot(xb, wg_ref[...], preferred_element_type=F32)
    up = jnp.dot(xb, wu_ref[...], preferred_element_type=F32)
    hid = gate * (1.0 / (1.0 + jnp.exp(-gate))) * up
    ffn = jnp.dot(hid.astype(BF16), wd_ref[...], preferred_element_type=F32)
    y = alpha * x + ffn
    out_ref[...] = _layer_norm(y, g_ref[...], b_ref[...])


def _full(shape):
    return pl.BlockSpec(shape, lambda *_: (0,) * len(shape))


def _params(sem):
    return pltpu.CompilerParams(dimension_semantics=sem, vmem_limit_bytes=VMEM_LIMIT)


def _pad_heads(w, dim):
    r = w.shape[0]
    w = w.reshape(r, N_HEADS, dim)
    return jnp.pad(w, ((0, 0), (0, 0), (0, HEAD_SLOT - dim))).reshape(r, N_HEADS * HEAD_SLOT)


def _rope_slot(w):
    return jnp.concatenate([w, w[..., :ROPE_HALF], jnp.zeros_like(w[..., :ROPE_HALF])], axis=-1)


def _layer(x, cos_t, sin_t, w_in, pool_w, pool_scale, q_norm_g, w_q_up, kv_norm_g, w_k_up, w_v_up,
           w_o, ln1_g, ln1_b, w_gate, w_up, w_down, ln2_g, ln2_b, alpha):
    bsz, seq, _ = x.shape
    tm = 512
    tq = 512
    o_kr = POOL_WIDTH + Q_LORA_RANK + KV_LORA_RANK

    w_in_ext = jnp.concatenate(
        [w_in[:, :o_kr], jnp.zeros((D_MODEL, ROPE_LO), w_in.dtype), _rope_slot(w_in[:, o_kr:])],
        axis=1).astype(BF16)
    wq = _pad_heads(w_q_up, QK_HEAD_DIM).T.astype(BF16)
    wk = _pad_heads(w_k_up, QK_NOPE_DIM).astype(BF16)
    wv = jnp.pad(w_v_up.T.reshape(N_HEADS, V_HEAD_DIM, KV_LORA_RANK),
                 ((0, 0), (0, V_SLOT - V_HEAD_DIM), (0, 0))).reshape(VT_ROWS, KV_LORA_RANK).astype(BF16)
    ones_col = (jnp.arange(VT_ROWS) % V_SLOT == V_HEAD_DIM).astype(F32)[:, None]
    q_scale = (QK_HEAD_DIM ** -0.5) * math.log2(math.e)

    tok = lambda w: pl.BlockSpec((1, tm, w), lambda b, i: (b, i, 0))
    rope = lambda: pl.BlockSpec((ROPE_HALF, tm), lambda b, i: (0, b * (seq // tm) + i))
    u, q, k, v = pl.pallas_call(
        functools.partial(_proj_kernel, q_scale=q_scale),
        grid=(bsz, seq // tm),
        in_specs=[tok(D_MODEL), rope(), rope(), _full((D_MODEL, H_WIDTH)),
                  _full((1, Q_LORA_RANK)), _full((N_HEADS * HEAD_SLOT, Q_LORA_RANK)),
                  _full((1, KV_LORA_RANK)), _full((KV_LORA_RANK, N_HEADS * HEAD_SLOT)),
                  _full((VT_ROWS, KV_LORA_RANK)), _full((VT_ROWS, 1))],
        out_specs=[tok(POOL_WIDTH),
                   pl.BlockSpec((1, N_HEADS * HEAD_SLOT, tm), lambda b, i: (b, 0, i)),
                   tok(N_HEADS * HEAD_SLOT),
                   pl.BlockSpec((1, VT_ROWS, tm), lambda b, i: (b, 0, i))],
        out_shape=[jax.ShapeDtypeStruct((bsz, seq, POOL_WIDTH), F32),
                   jax.ShapeDtypeStruct((bsz, N_HEADS * HEAD_SLOT, seq), BF16),
                   jax.ShapeDtypeStruct((bsz, seq, N_HEADS * HEAD_SLOT), BF16),
                   jax.ShapeDtypeStruct((bsz, VT_ROWS, seq), BF16)],
        compiler_params=_params(("parallel", "parallel")),
        name="proj",
    )(x, cos_t, sin_t, w_in_ext, q_norm_g[None, :], wq, kv_norm_g[None, :], wk, wv, ones_col)

    hp = 4
    attn = pl.pallas_call(
        functools.partial(_attn_kernel, kc=256),
        grid=(bsz, N_HEADS // hp, seq // tq),
        in_specs=[pl.BlockSpec((1, hp * HEAD_SLOT, tq), lambda b, g, i: (b, g, i)),
                  pl.BlockSpec((1, hp * HEAD_SLOT, tq),
                               lambda b, g, i: (b, g, jnp.minimum(i + 1, seq // tq - 1))),
                  pl.BlockSpec((1, seq, hp * HEAD_SLOT), lambda b, g, i: (b, 0, g)),
                  pl.BlockSpec((1, hp * V_SLOT, seq), lambda b, g, i: (b, g, 0)),
                  _full((SUBLANES, tq))],
        out_specs=pl.BlockSpec((1, tq, hp * V_HEAD_DIM), lambda b, g, i: (b, i, g)),
        out_shape=jax.ShapeDtypeStruct((bsz, seq, ATTN_WIDTH), BF16),
        scratch_shapes=[pltpu.VMEM((2, seq, tq), F32),
                        pltpu.VMEM((SUBLANES, tq), F32)],
        compiler_params=_params(("parallel", "parallel", "arbitrary")),
        name="attn",
    )(q, q, k, v, jnp.zeros((SUBLANES, tq), jnp.int32))

    halo_blocks = tm // POOL_HALO
    last_halo = seq // POOL_HALO - 1
    x1 = pl.pallas_call(
        functools.partial(_mix_kernel, seq=seq, alpha=alpha),
        grid=(bsz, seq // tm),
        in_specs=[tok(POOL_WIDTH),
                  pl.BlockSpec((1, POOL_HALO, POOL_WIDTH),
                               lambda b, i: (b, jnp.maximum(i * halo_blocks - 1, 0), 0)),
                  pl.BlockSpec((1, POOL_HALO, POOL_WIDTH),
                               lambda b, i: (b, jnp.minimum((i + 1) * halo_blocks, last_halo), 0)),
                  tok(ATTN_WIDTH), tok(D_MODEL),
                  _full((len(POOL_WINDOWS), POOL_GROUP_DIM, POOL_GROUP_DIM)), _full((1, POOL_WIDTH)),
                  _full((D_MODEL, D_MODEL)), _full((1, D_MODEL)), _full((1, D_MODEL))],
        out_specs=tok(D_MODEL),
        out_shape=jax.ShapeDtypeStruct((bsz, seq, D_MODEL), F32),
        scratch_shapes=[pltpu.VMEM((tm + 2 * POOL_HALO, POOL_WIDTH), F32)],
        compiler_params=_params(("parallel", "arbitrary")),
        name="mix",
    )(u, u, u, attn, x, pool_w.astype(BF16), pool_scale[None, :], w_o.astype(BF16),
      ln1_g[None, :], ln1_b[None, :])

    tf = 256
    rows = bsz * seq
    wspec = lambda shape: pl.BlockSpec(shape, lambda i: (0, 0), pipeline_mode=pl.Buffered(1))
    x2 = pl.pallas_call(
        functools.partial(_ffn_kernel, alpha=alpha),
        grid=(rows // tf,),
        in_specs=[pl.BlockSpec((tf, D_MODEL), lambda i: (i, 0)),
                  wspec((D_MODEL, FFN_HIDDEN)), wspec((D_MODEL, FFN_HIDDEN)), wspec((FFN_HIDDEN, D_MODEL)),
                  _full((1, D_MODEL)), _full((1, D_MODEL))],
        out_specs=pl.BlockSpec((tf, D_MODEL), lambda i: (i, 0)),
        out_shape=jax.ShapeDtypeStruct((rows, D_MODEL), F32),
        compiler_params=_params(("parallel",)),
        name="ffn",
    )(x1.reshape(rows, D_MODEL), w_gate.astype(BF16), w_up.astype(BF16), w_down.astype(BF16),
      ln2_g[None, :], ln2_b[None, :])
    return x2.reshape(bsz, seq, D_MODEL)


def kernel(x, positions, w_in, pool_w, pool_scale, q_norm_g, w_q_up, kv_norm_g, w_k_up, w_v_up, w_o, ln1_g, ln1_b, w_gate, w_up, w_down, ln2_g, ln2_b):
    depth = w_in.shape[0]
    alpha = (2.0 * depth) ** 0.25
    inv_freq = 1.0 / (ROPE_THETA ** (jnp.arange(0, QK_ROPE_DIM, 2, dtype=F32) / QK_ROPE_DIM))
    tokens = positions.size
    plane = (ROPE_HALF, tokens // LANES, LANES)
    cos_t, sin_t = pl.pallas_call(
        _rope_table_kernel,
        in_specs=[pl.BlockSpec(memory_space=pltpu.SMEM), _full(plane[1:])],
        out_specs=[_full(plane), _full(plane)],
        out_shape=[jax.ShapeDtypeStruct(plane, F32)] * 2,
        grid=(1,),
        name="rope_table",
    )(inv_freq, positions.reshape(plane[1:]))
    cos_t = cos_t.reshape(ROPE_HALF, tokens)
    sin_t = sin_t.reshape(ROPE_HALF, tokens)
    for l in range(depth):
        x = _layer(x, cos_t, sin_t, w_in[l], pool_w[l], pool_scale[l], q_norm_g[l], w_q_up[l],
                   kv_norm_g[l], w_k_up[l], w_v_up[l], w_o[l], ln1_g[l], ln1_b[l],
                   w_gate[l], w_up[l], w_down[l], ln2_g[l], ln2_b[l], alpha)
    return x
```

```python
import functools
import itertools
import math

import jax
import jax.numpy as jnp
from jax import lax
from jax.experimental import pallas as pl
from jax.experimental.pallas import tpu as pltpu

D_MODEL = 1024
POOL_WIDTH = 512
POOL_WINDOWS = (2, 4, 8, 16)
POOL_GROUP_DIM = 128
N_HEADS = 8
QK_NOPE_DIM = 64
QK_ROPE_DIM = 32
V_HEAD_DIM = 64
QK_HEAD_DIM = QK_NOPE_DIM + QK_ROPE_DIM
ATTN_WIDTH = N_HEADS * V_HEAD_DIM
Q_LORA_RANK = 384
KV_LORA_RANK = 256
ROPE_THETA = 10000.0
FFN_HIDDEN = 2816
LN_EPS = 1e-5
RMS_EPS = 1e-6

LANES = 128
SUBLANES = 8
BF16_ROWS = 2 * SUBLANES
V_SLOT = V_HEAD_DIM + BF16_ROWS
VT_ROWS = N_HEADS * V_SLOT
HEAD_SLOT = LANES
ROPE_LO = QK_NOPE_DIM
ROPE_HALF = QK_ROPE_DIM // 2
H_WIDTH = POOL_WIDTH + Q_LORA_RANK + KV_LORA_RANK + LANES
POOL_HALO = 8
VMEM_LIMIT = 56 * 1024 * 1024

BF16 = jnp.bfloat16
F32 = jnp.float32


def _rope(t, cos, sin_signed):
    partner = pltpu.roll(t, LANES - ROPE_HALF, 1)
    return t * cos + partner * sin_signed


def _rms(t, g):
    y = t * lax.rsqrt(jnp.mean(t * t, axis=-1, keepdims=True) + RMS_EPS)
    return y * g


def _layer_norm(y, g, b):
    mu = jnp.mean(y, axis=-1, keepdims=True)
    d = y - mu
    var = jnp.mean(d * d, axis=-1, keepdims=True)
    return d * lax.rsqrt(var + LN_EPS) * g + b


def _rope_table_kernel(invf_ref, pos_ref, cos_ref, sin_ref):
    pos = pos_ref[...].astype(F32)
    for f in range(ROPE_HALF):
        ang = pos * invf_ref[f]
        cos_ref[f] = jnp.cos(ang)
        sin_ref[f] = jnp.sin(ang)


def _rope_lanes(t, outside):
    tm = t.shape[1]
    full = jnp.concatenate([jnp.full((ROPE_LO, tm), outside, F32), t, t,
                            jnp.zeros((LANES - ROPE_LO - QK_ROPE_DIM, tm), F32)], axis=0)
    return full.T


def _proj_kernel(x_ref, cos_ref, sin_ref, w_in_ref, qg_ref, wq_ref, kvg_ref, wk_ref, wv_ref, ones_ref,
                 u_ref, q_ref, k_ref, v_ref, *, q_scale):
    x = x_ref[0].astype(BF16)
    h = jnp.dot(x, w_in_ref[...], preferred_element_type=F32)
    u_ref[0] = h[:, :POOL_WIDTH]
    o_kv = POOL_WIDTH + Q_LORA_RANK
    o_kr = o_kv + KV_LORA_RANK
    cq = _rms(h[:, POOL_WIDTH:o_kv], qg_ref[...])
    ckv = _rms(h[:, o_kv:o_kr], kvg_ref[...])
    kr = h[:, o_kr:]

    cos_t = cos_ref[...]
    sin_t = sin_ref[...]
    cos = _rope_lanes(cos_t, 1.0)
    sin = _rope_lanes(sin_t, 0.0)
    first_half = lax.broadcasted_iota(jnp.int32, cos.shape, 1) < ROPE_LO + ROPE_HALF
    sin = jnp.where(first_half, -sin, sin)

    cq_b = cq.astype(BF16)
    ckv_b = ckv.astype(BF16)
    nt_dims = (((1,), (1,)), ((), ()))
    qt = lax.dot_general(wq_ref[...], cq_b, nt_dims, preferred_element_type=F32)
    k_nope = jnp.dot(ckv_b, wk_ref[...], preferred_element_type=F32)
    vt = lax.dot_general(wv_ref[...], ckv_b, nt_dims, preferred_element_type=F32)
    v_ref[0] = (vt + ones_ref[...]).astype(BF16)

    cos_q = cos_t * q_scale
    sin_q = sin_t * q_scale
    k_rope = _rope(kr, cos, sin)
    for hd in range(N_HEADS):
        sl = slice(hd * HEAD_SLOT, (hd + 1) * HEAD_SLOT)
        k_ref[0, :, sl] = (k_nope[:, sl] + k_rope).astype(BF16)
        base = hd * HEAD_SLOT
        r1 = qt[base + ROPE_LO:base + ROPE_LO + ROPE_HALF]
        r2 = qt[base + ROPE_LO + ROPE_HALF:base + ROPE_LO + QK_ROPE_DIM]
        q_ref[0, base:base + ROPE_LO, :] = (qt[base:base + ROPE_LO] * q_scale).astype(BF16)
        q_ref[0, base + ROPE_LO:base + HEAD_SLOT, :] = jnp.concatenate(
            [r1 * cos_q - r2 * sin_q, r2 * cos_q + r1 * sin_q,
             jnp.zeros((HEAD_SLOT - QK_HEAD_DIM, qt.shape[1]), F32)], axis=0).astype(BF16)


def _row_groups(t, op):
    return op(t.reshape(t.shape[0] // SUBLANES, SUBLANES, t.shape[1]), axis=0)


SKEW = 4


def _attn_kernel(q_ref, qn_ref, k_ref, vt_ref, zero_ref, o_ref, s_ref, mx_ref, *, kc):
    seq = k_ref.shape[1]
    hp = k_ref.shape[2] // HEAD_SLOT
    n_chunks = seq // kc
    assert hp % 2 == 0

    def score_chunk(q, h, c):
        hs = slice(h * HEAD_SLOT, (h + 1) * HEAD_SLOT)
        ks = slice(c * kc, (c + 1) * kc)
        s = jnp.dot(k_ref[0, ks, hs], q[0, hs, :], preferred_element_type=F32)
        s_ref[h % 2, ks, :] = s
        return _row_groups(s, jnp.max)

    def fold(cms):
        mx = cms[0]
        for cm in cms[1:]:
            mx = jnp.maximum(mx, cm)
        return mx

    @pl.when(pl.program_id(2) == 0)
    def _():
        mx_ref[...] = fold([score_chunk(q_ref, 0, c) for c in range(n_chunks)])

    mx = mx_ref[...]
    zero_bits = zero_ref[0:1, :]
    outs = []
    for h in range(hp):
        m = jnp.max(mx, axis=0, keepdims=True)
        ds = slice(h * V_SLOT, (h + 1) * V_SLOT)
        cms = []
        acc = None
        for c in range(n_chunks):
            ks = slice(c * kc, (c + 1) * kc)
            cms.append(score_chunk(q_ref, h + 1, c) if h + 1 < hp else score_chunk(qn_ref, 0, c))
            m_c = m
            if c >= SKEW:
                dep = lax.bitcast_convert_type(cms[c - SKEW][0:1, :], jnp.int32) & zero_bits
                m_c = m + lax.bitcast_convert_type(dep, F32)
            p = jnp.exp2(s_ref[h % 2, ks, :] - m_c)
            o = jnp.dot(vt_ref[0, ds, ks], p.astype(BF16), preferred_element_type=F32)
            acc = o if acc is None else acc + o
        outs.append(acc[:V_HEAD_DIM] / acc[V_HEAD_DIM:V_HEAD_DIM + 1])
        mx = fold(cms)
    mx_ref[...] = mx
    o_ref[0] = jnp.concatenate(outs, axis=0).T.astype(BF16)


def _mix_stages(emit, i, n_tiles, seq, alpha, u_ref, uprev_ref, unext_ref, attn_ref, x_ref,
                pw_ref, ps_ref, wo_ref, g_ref, b_ref, ext_ref):
    tm = u_ref.shape[0]
    ext_ref[0:POOL_HALO] = jnp.where(i > 0, uprev_ref[...], 0.0)
    ext_ref[POOL_HALO:POOL_HALO + tm] = u_ref[...]
    ext_ref[POOL_HALO + tm:] = jnp.where(i < n_tiles - 1, unext_ref[...], 0.0)
    row = i * tm + lax.broadcasted_iota(jnp.int32, (tm, 1), 0)
    mixed = []
    for g, w in enumerate(POOL_WINDOWS):
        cols = slice(g * POOL_GROUP_DIM, (g + 1) * POOL_GROUP_DIM)
        back = w // 2
        win = ext_ref[pl.ds(POOL_HALO - back, tm), cols]
        for d in range(1 - back, w - back):
            win = win + ext_ref[pl.ds(POOL_HALO + d, tm), cols]
        cnt = (jnp.minimum(row + (w - back), seq) - jnp.maximum(row - back, 0)).astype(F32)
        pg = win / cnt - ext_ref[pl.ds(POOL_HALO, tm), cols]
        og = jnp.dot(pg.astype(BF16), pw_ref[g], preferred_element_type=F32) * ps_ref[:, cols]
        mixed.append(og.astype(BF16))
        yield
    mixed.append(attn_ref[...])
    acc = jnp.dot(jnp.concatenate(mixed, axis=1), wo_ref[...], preferred_element_type=F32)
    y = alpha * x_ref[...] + acc
    emit(_layer_norm(y, g_ref[...], b_ref[...]))
    yield


FFN_CHUNKS = ((0, 768), (768, 1536), (1536, 2304), (2304, FFN_HIDDEN))


def _ffn_stages(emit, x, alpha, wg_ref, wu_ref, wd_ref, g_ref, b_ref):
    xb = x.astype(BF16)
    ffn = None
    for lo, hi in FFN_CHUNKS:
        gate = jnp.dot(xb, wg_ref[:, lo:hi], preferred_element_type=F32)
        up = jnp.dot(xb, wu_ref[:, lo:hi], preferred_element_type=F32)
        hid = gate * (1.0 / (1.0 + jnp.exp(-gate))) * up
        part = jnp.dot(hid.astype(BF16), wd_ref[lo:hi, :], preferred_element_type=F32)
        ffn = part if ffn is None else ffn + part
        yield
    y = alpha * x + ffn
    emit(_layer_norm(y, g_ref[...], b_ref[...]))
    yield


def _mixffn_kernel(u_ref, uprev_ref, unext_ref, attn_ref, x_ref,
                   u0_ref, unext0_ref, attn0_ref, x0_ref,
                   pw_ref, ps_ref, wo_ref, g1_ref, b1_ref, wg_ref, wu_ref, wd_ref, g2_ref, b2_ref,
                   out_ref, ext_ref, x1_ref, *, seq, alpha):
    s = pl.program_id(0)
    n_steps = pl.num_programs(0)
    tm = u_ref.shape[0]
    n_tiles = seq // tm
    mix_w = (pw_ref, ps_ref, wo_ref, g1_ref, b1_ref, ext_ref)

    def store(ref, idx):
        def emit(v):
            ref[idx] = v
        return emit

    @pl.when(s == 0)
    def _():
        for _ in _mix_stages(store(x1_ref, 0), 0, n_tiles, seq, alpha, u0_ref, unext0_ref, unext0_ref,
                             attn0_ref, x0_ref, *mix_w):
            pass

    nxt = jnp.minimum(s + 1, n_steps - 1)
    ffn = _ffn_stages(store(out_ref, ...), x1_ref[s % 2], alpha, wg_ref, wu_ref, wd_ref, g2_ref, b2_ref)
    mix = _mix_stages(store(x1_ref, (s + 1) % 2), nxt % n_tiles, n_tiles, seq, alpha,
                      u_ref, uprev_ref, unext_ref, attn_ref, x_ref, *mix_w)
    for _ in itertools.zip_longest(ffn, mix):
        pass


def _full(shape):
    return pl.BlockSpec(shape, lambda *_: (0,) * len(shape))


def _params(sem):
    return pltpu.CompilerParams(dimension_semantics=sem, vmem_limit_bytes=VMEM_LIMIT)


def _pad_heads(w, dim):
    r = w.shape[0]
    w = w.reshape(r, N_HEADS, dim)
    return jnp.pad(w, ((0, 0), (0, 0), (0, HEAD_SLOT - dim))).reshape(r, N_HEADS * HEAD_SLOT)


def _rope_slot(w):
    return jnp.concatenate([w, w[..., :ROPE_HALF], jnp.zeros_like(w[..., :ROPE_HALF])], axis=-1)


def _layer(x, cos_t, sin_t, w_in, pool_w, pool_scale, q_norm_g, w_q_up, kv_norm_g, w_k_up, w_v_up,
           w_o, ln1_g, ln1_b, w_gate, w_up, w_down, ln2_g, ln2_b, alpha):
    bsz, seq, _ = x.shape
    tm = 512
    tq = 512
    o_kr = POOL_WIDTH + Q_LORA_RANK + KV_LORA_RANK

    w_in_ext = jnp.concatenate(
        [w_in[:, :o_kr], jnp.zeros((D_MODEL, ROPE_LO), w_in.dtype), _rope_slot(w_in[:, o_kr:])],
        axis=1).astype(BF16)
    wq = _pad_heads(w_q_up, QK_HEAD_DIM).T.astype(BF16)
    wk = _pad_heads(w_k_up, QK_NOPE_DIM).astype(BF16)
    wv = jnp.pad(w_v_up.T.reshape(N_HEADS, V_HEAD_DIM, KV_LORA_RANK),
                 ((0, 0), (0, V_SLOT - V_HEAD_DIM), (0, 0))).reshape(VT_ROWS, KV_LORA_RANK).astype(BF16)
    ones_col = (jnp.arange(VT_ROWS) % V_SLOT == V_HEAD_DIM).astype(F32)[:, None]
    q_scale = (QK_HEAD_DIM ** -0.5) * math.log2(math.e)

    tok = lambda w: pl.BlockSpec((1, tm, w), lambda b, i: (b, i, 0))
    rope = lambda: pl.BlockSpec((ROPE_HALF, tm), lambda b, i: (0, b * (seq // tm) + i))
    u, q, k, v = pl.pallas_call(
        functools.partial(_proj_kernel, q_scale=q_scale),
        grid=(bsz, seq // tm),
        in_specs=[tok(D_MODEL), rope(), rope(), _full((D_MODEL, H_WIDTH)),
                  _full((1, Q_LORA_RANK)), _full((N_HEADS * HEAD_SLOT, Q_LORA_RANK)),
                  _full((1, KV_LORA_RANK)), _full((KV_LORA_RANK, N_HEADS * HEAD_SLOT)),
                  _full((VT_ROWS, KV_LORA_RANK)), _full((VT_ROWS, 1))],
        out_specs=[tok(POOL_WIDTH),
                   pl.BlockSpec((1, N_HEADS * HEAD_SLOT, tm), lambda b, i: (b, 0, i)),
                   tok(N_HEADS * HEAD_SLOT),
                   pl.BlockSpec((1, VT_ROWS, tm), lambda b, i: (b, 0, i))],
        out_shape=[jax.ShapeDtypeStruct((bsz, seq, POOL_WIDTH), F32),
                   jax.ShapeDtypeStruct((bsz, N_HEADS * HEAD_SLOT, seq), BF16),
                   jax.ShapeDtypeStruct((bsz, seq, N_HEADS * HEAD_SLOT), BF16),
                   jax.ShapeDtypeStruct((bsz, VT_ROWS, seq), BF16)],
        compiler_params=_params(("parallel", "parallel")),
        name="proj",
    )(x, cos_t, sin_t, w_in_ext, q_norm_g[None, :], wq, kv_norm_g[None, :], wk, wv, ones_col)

    hp = 4
    attn = pl.pallas_call(
        functools.partial(_attn_kernel, kc=256),
        grid=(bsz, N_HEADS // hp, seq // tq),
        in_specs=[pl.BlockSpec((1, hp * HEAD_SLOT, tq), lambda b, g, i: (b, g, i)),
                  pl.BlockSpec((1, hp * HEAD_SLOT, tq),
                               lambda b, g, i: (b, g, jnp.minimum(i + 1, seq // tq - 1))),
                  pl.BlockSpec((1, seq, hp * HEAD_SLOT), lambda b, g, i: (b, 0, g)),
                  pl.BlockSpec((1, hp * V_SLOT, seq), lambda b, g, i: (b, g, 0)),
                  _full((SUBLANES, tq))],
        out_specs=pl.BlockSpec((1, tq, hp * V_HEAD_DIM), lambda b, g, i: (b, i, g)),
        out_shape=jax.ShapeDtypeStruct((bsz, seq, ATTN_WIDTH), BF16),
        scratch_shapes=[pltpu.VMEM((2, seq, tq), F32),
                        pltpu.VMEM((SUBLANES, tq), F32)],
        compiler_params=_params(("parallel", "parallel", "arbitrary")),
        name="attn",
    )(q, q, k, v, jnp.zeros((SUBLANES, tq), jnp.int32))

    tf = 256
    rows = bsz * seq
    n_steps = rows // tf
    halo_blocks = tf // POOL_HALO
    last_halo = rows // POOL_HALO - 1
    ahead = lambda s: jnp.minimum(s + 1, n_steps - 1)
    row_blk = lambda w, idx: pl.BlockSpec((tf, w), lambda s: (idx(s), 0))
    halo = lambda idx: pl.BlockSpec((POOL_HALO, POOL_WIDTH), lambda s: (idx(s), 0))
    first = lambda s: 0
    once = lambda shape: pl.BlockSpec(shape, lambda s: (0,) * len(shape), pipeline_mode=pl.Buffered(1))
    u2 = u.reshape(rows, POOL_WIDTH)
    attn2 = attn.reshape(rows, ATTN_WIDTH)
    xr = x.reshape(rows, D_MODEL)
    x2 = pl.pallas_call(
        functools.partial(_mixffn_kernel, seq=seq, alpha=alpha),
        grid=(n_steps,),
        in_specs=[row_blk(POOL_WIDTH, ahead),
                  halo(lambda s: jnp.maximum(ahead(s) * halo_blocks - 1, 0)),
                  halo(lambda s: jnp.minimum((ahead(s) + 1) * halo_blocks, last_halo)),
                  row_blk(ATTN_WIDTH, ahead), row_blk(D_MODEL, ahead),
                  row_blk(POOL_WIDTH, first), halo(lambda s: halo_blocks),
                  row_blk(ATTN_WIDTH, first), row_blk(D_MODEL, first),
                  once((len(POOL_WINDOWS), POOL_GROUP_DIM, POOL_GROUP_DIM)), once((1, POOL_WIDTH)),
                  once((D_MODEL, D_MODEL)), once((1, D_MODEL)), once((1, D_MODEL)),
                  once((D_MODEL, FFN_HIDDEN)), once((D_MODEL, FFN_HIDDEN)), once((FFN_HIDDEN, D_MODEL)),
                  once((1, D_MODEL)), once((1, D_MODEL))],
        out_specs=pl.BlockSpec((tf, D_MODEL), lambda s: (s, 0)),
        out_shape=jax.ShapeDtypeStruct((rows, D_MODEL), F32),
        scratch_shapes=[pltpu.VMEM((tf + 2 * POOL_HALO, POOL_WIDTH), F32),
                        pltpu.VMEM((2, tf, D_MODEL), F32)],
        compiler_params=_params(("arbitrary",)),
        name="mixffn",
    )(u2, u2, u2, attn2, xr, u2, u2, attn2, xr,
      pool_w.astype(BF16), pool_scale[None, :], w_o.astype(BF16), ln1_g[None, :], ln1_b[None, :],
      w_gate.astype(BF16), w_up.astype(BF16), w_down.astype(BF16), ln2_g[None, :], ln2_b[None, :])
    return x2.reshape(bsz, seq, D_MODEL)


def kernel(x, positions, w_in, pool_w, pool_scale, q_norm_g, w_q_up, kv_norm_g, w_k_up, w_v_up, w_o, ln1_g, ln1_b, w_gate, w_up, w_down, ln2_g, ln2_b):
    depth = w_in.shape[0]
    alpha = (2.0 * depth) ** 0.25
    inv_freq = 1.0 / (ROPE_THETA ** (jnp.arange(0, QK_ROPE_DIM, 2, dtype=F32) / QK_ROPE_DIM))
    tokens = positions.size
    plane = (ROPE_HALF, tokens // LANES, LANES)
    cos_t, sin_t = pl.pallas_call(
        _rope_table_kernel,
        in_specs=[pl.BlockSpec(memory_space=pltpu.SMEM), _full(plane[1:])],
        out_specs=[_full(plane), _full(plane)],
        out_shape=[jax.ShapeDtypeStruct(plane, F32)] * 2,
        grid=(1,),
        name="rope_table",
    )(inv_freq, positions.reshape(plane[1:]))
    cos_t = cos_t.reshape(ROPE_HALF, tokens)
    sin_t = sin_t.reshape(ROPE_HALF, tokens)
    for l in range(depth):
        x = _layer(x, cos_t, sin_t, w_in[l], pool_w[l], pool_scale[l], q_norm_g[l], w_q_up[l],
                   kv_norm_g[l], w_k_up[l], w_v_up[l], w_o[l], ln1_g[l], ln1_b[l],
                   w_gate[l], w_up[l], w_down[l], ln2_g[l], ln2_b[l], alpha)
    return x
```

```python
import functools
import itertools
import math

import jax
import jax.numpy as jnp
from jax import lax
from jax.experimental import pallas as pl
from jax.experimental.pallas import tpu as pltpu

D_MODEL = 1024
POOL_WIDTH = 512
POOL_WINDOWS = (2, 4, 8, 16)
POOL_GROUP_DIM = 128
N_HEADS = 8
QK_NOPE_DIM = 64
QK_ROPE_DIM = 32
V_HEAD_DIM = 64
QK_HEAD_DIM = QK_NOPE_DIM + QK_ROPE_DIM
ATTN_WIDTH = N_HEADS * V_HEAD_DIM
Q_LORA_RANK = 384
KV_LORA_RANK = 256
ROPE_THETA = 10000.0
FFN_HIDDEN = 2816
LN_EPS = 1e-5
RMS_EPS = 1e-6

LANES = 128
SUBLANES = 8
BF16_ROWS = 2 * SUBLANES
V_SLOT = V_HEAD_DIM + BF16_ROWS
VT_ROWS = N_HEADS * V_SLOT
HEAD_SLOT = LANES
ROPE_LO = QK_NOPE_DIM
ROPE_HALF = QK_ROPE_DIM // 2
H_WIDTH = POOL_WIDTH + Q_LORA_RANK + KV_LORA_RANK + LANES
POOL_HALO = 8
VMEM_LIMIT = 56 * 1024 * 1024

BF16 = jnp.bfloat16
F32 = jnp.float32


def _rope(t, cos, sin_signed):
    partner = pltpu.roll(t, LANES - ROPE_HALF, 1)
    return t * cos + partner * sin_signed


def _rms(t, g):
    y = t * lax.rsqrt(jnp.mean(t * t, axis=-1, keepdims=True) + RMS_EPS)
    return y * g


def _layer_norm(y, g, b):
    mu = jnp.mean(y, axis=-1, keepdims=True)
    d = y - mu
    var = jnp.mean(d * d, axis=-1, keepdims=True)
    return d * lax.rsqrt(var + LN_EPS) * g + b


def _rope_table_kernel(invf_ref, pos_ref, cos_ref, sin_ref):
    pos = pos_ref[...].astype(F32)
    for f in range(ROPE_HALF):
        ang = pos * invf_ref[f]
        cos_ref[f] = jnp.cos(ang)
        sin_ref[f] = jnp.sin(ang)


def _rope_lanes(t, outside):
    tm = t.shape[1]
    full = jnp.concatenate([jnp.full((ROPE_LO, tm), outside, F32), t, t,
                            jnp.zeros((LANES - ROPE_LO - QK_ROPE_DIM, tm), F32)], axis=0)
    return full.T


def _proj_kernel(x_ref, cos_ref, sin_ref, w_in_ref, qg_ref, wq_ref, kvg_ref, wk_ref, wv_ref, ones_ref,
                 u_ref, q_ref, k_ref, v_ref, *, q_scale):
    x = x_ref[0].astype(BF16)
    h = jnp.dot(x, w_in_ref[...], preferred_element_type=F32)
    u_ref[0] = h[:, :POOL_WIDTH]
    o_kv = POOL_WIDTH + Q_LORA_RANK
    o_kr = o_kv + KV_LORA_RANK
    cq = _rms(h[:, POOL_WIDTH:o_kv], qg_ref[...])
    ckv = _rms(h[:, o_kv:o_kr], kvg_ref[...])
    kr = h[:, o_kr:]

    cos_t = cos_ref[...]
    sin_t = sin_ref[...]
    cos = _rope_lanes(cos_t, 1.0)
    sin = _rope_lanes(sin_t, 0.0)
    first_half = lax.broadcasted_iota(jnp.int32, cos.shape, 1) < ROPE_LO + ROPE_HALF
    sin = jnp.where(first_half, -sin, sin)

    cq_b = cq.astype(BF16)
    ckv_b = ckv.astype(BF16)
    nt_dims = (((1,), (1,)), ((), ()))
    qt = lax.dot_general(wq_ref[...], cq_b, nt_dims, preferred_element_type=F32)
    k_nope = jnp.dot(ckv_b, wk_ref[...], preferred_element_type=F32)
    vt = lax.dot_general(wv_ref[...], ckv_b, nt_dims, preferred_element_type=F32)
    v_ref[0] = (vt + ones_ref[...]).astype(BF16)

    cos_q = cos_t * q_scale
    sin_q = sin_t * q_scale
    k_rope = _rope(kr, cos, sin)
    for hd in range(N_HEADS):
        sl = slice(hd * HEAD_SLOT, (hd + 1) * HEAD_SLOT)
        k_ref[0, :, sl] = (k_nope[:, sl] + k_rope).astype(BF16)
        base = hd * HEAD_SLOT
        r1 = qt[base + ROPE_LO:base + ROPE_LO + ROPE_HALF]
        r2 = qt[base + ROPE_LO + ROPE_HALF:base + ROPE_LO + QK_ROPE_DIM]
        q_ref[0, base:base + ROPE_LO, :] = (qt[base:base + ROPE_LO] * q_scale).astype(BF16)
        q_ref[0, base + ROPE_LO:base + HEAD_SLOT, :] = jnp.concatenate(
            [r1 * cos_q - r2 * sin_q, r2 * cos_q + r1 * sin_q,
             jnp.zeros((HEAD_SLOT - QK_HEAD_DIM, qt.shape[1]), F32)], axis=0).astype(BF16)


def _row_groups(t, op):
    return op(t.reshape(t.shape[0] // SUBLANES, SUBLANES, t.shape[1]), axis=0)


SKEW = 3


def _attn_kernel(q_ref, qn_ref, k_ref, vt_ref, zero_ref, o_ref, s_ref, mx_ref, *, kc):
    seq = k_ref.shape[1]
    hp = k_ref.shape[2] // HEAD_SLOT
    n_chunks = seq // kc
    assert hp % 2 == 0

    def score_chunk(q, h, c):
        hs = slice(h * HEAD_SLOT, (h + 1) * HEAD_SLOT)
        ks = slice(c * kc, (c + 1) * kc)
        s = jnp.dot(k_ref[0, ks, hs], q[0, hs, :], preferred_element_type=F32)
        s_ref[h % 2, ks, :] = s
        return _row_groups(s, jnp.max)

    def fold(cms):
        mx = cms[0]
        for cm in cms[1:]:
            mx = jnp.maximum(mx, cm)
        return mx

    @pl.when(pl.program_id(2) == 0)
    def _():
        mx_ref[...] = fold([score_chunk(q_ref, 0, c) for c in range(n_chunks)])

    mx = mx_ref[...]
    zero_bits = zero_ref[0:1, :]
    outs = []
    for h in range(hp):
        m = jnp.max(mx, axis=0, keepdims=True)
        ds = slice(h * V_SLOT, (h + 1) * V_SLOT)
        cms = []
        acc = None
        for c in range(n_chunks):
            ks = slice(c * kc, (c + 1) * kc)
            cms.append(score_chunk(q_ref, h + 1, c) if h + 1 < hp else score_chunk(qn_ref, 0, c))
            m_c = m
            if c >= SKEW:
                dep = lax.bitcast_convert_type(cms[c - SKEW][0:1, :], jnp.int32) & zero_bits
                m_c = m + lax.bitcast_convert_type(dep, F32)
            p = jnp.exp2(s_ref[h % 2, ks, :] - m_c)
            o = jnp.dot(vt_ref[0, ds, ks], p.astype(BF16), preferred_element_type=F32)
            acc = o if acc is None else acc + o
        outs.append(acc[:V_HEAD_DIM] / acc[V_HEAD_DIM:V_HEAD_DIM + 1])
        mx = fold(cms)
    mx_ref[...] = mx
    o_ref[0] = jnp.concatenate(outs, axis=0).T.astype(BF16)


def _mix_stages(emit, i, n_tiles, seq, alpha, u_ref, uprev_ref, unext_ref, attn_ref, x_ref,
                pw_ref, ps_ref, wo_ref, g_ref, b_ref, ext_ref):
    tm = u_ref.shape[0]
    ext_ref[0:POOL_HALO] = jnp.where(i > 0, uprev_ref[...], 0.0)
    ext_ref[POOL_HALO:POOL_HALO + tm] = u_ref[...]
    ext_ref[POOL_HALO + tm:] = jnp.where(i < n_tiles - 1, unext_ref[...], 0.0)
    row = i * tm + lax.broadcasted_iota(jnp.int32, (tm, 1), 0)
    mixed = []
    for g, w in enumerate(POOL_WINDOWS):
        cols = slice(g * POOL_GROUP_DIM, (g + 1) * POOL_GROUP_DIM)
        back = w // 2
        win = ext_ref[pl.ds(POOL_HALO - back, tm), cols]
        for d in range(1 - back, w - back):
            win = win + ext_ref[pl.ds(POOL_HALO + d, tm), cols]
        cnt = (jnp.minimum(row + (w - back), seq) - jnp.maximum(row - back, 0)).astype(F32)
        pg = win / cnt - ext_ref[pl.ds(POOL_HALO, tm), cols]
        og = jnp.dot(pg.astype(BF16), pw_ref[g], preferred_element_type=F32) * ps_ref[:, cols]
        mixed.append(og.astype(BF16))
        yield
    mixed.append(attn_ref[...])
    acc = jnp.dot(jnp.concatenate(mixed, axis=1), wo_ref[...], preferred_element_type=F32)
    y = alpha * x_ref[...] + acc
    emit(_layer_norm(y, g_ref[...], b_ref[...]))
    yield


FFN_CHUNKS = ((0, 768), (768, 1536), (1536, 2304), (2304, FFN_HIDDEN))


def _ffn_stages(emit, x, alpha, wg_ref, wu_ref, wd_ref, g_ref, b_ref):
    xb = x.astype(BF16)
    ffn = None
    for lo, hi in FFN_CHUNKS:
        gate = jnp.dot(xb, wg_ref[:, lo:hi], preferred_element_type=F32)
        up = jnp.dot(xb, wu_ref[:, lo:hi], preferred_element_type=F32)
        hid = gate * (1.0 / (1.0 + jnp.exp(-gate))) * up
        part = jnp.dot(hid.astype(BF16), wd_ref[lo:hi, :], preferred_element_type=F32)
        ffn = part if ffn is None else ffn + part
        yield
    y = alpha * x + ffn
    emit(_layer_norm(y, g_ref[...], b_ref[...]))
    yield


MIXFFN_ORDER = "fmfmfmfmfm"


def _mixffn_kernel(u_ref, uprev_ref, unext_ref, attn_ref, x_ref,
                   u0_ref, unext0_ref, attn0_ref, x0_ref,
                   pw_ref, ps_ref, wo_ref, g1_ref, b1_ref, wg_ref, wu_ref, wd_ref, g2_ref, b2_ref,
                   out_ref, ext_ref, x1_ref, *, seq, alpha):
    s = pl.program_id(0)
    n_steps = pl.num_programs(0)
    tm = u_ref.shape[0]
    n_tiles = seq // tm
    mix_w = (pw_ref, ps_ref, wo_ref, g1_ref, b1_ref, ext_ref)

    def store(ref, idx):
        def emit(v):
            ref[idx] = v
        return emit

    @pl.when(s == 0)
    def _():
        for _ in _mix_stages(store(x1_ref, 0), 0, n_tiles, seq, alpha, u0_ref, unext0_ref, unext0_ref,
                             attn0_ref, x0_ref, *mix_w):
            pass

    nxt = jnp.minimum(s + 1, n_steps - 1)
    ffn = _ffn_stages(store(out_ref, ...), x1_ref[s % 2], alpha, wg_ref, wu_ref, wd_ref, g2_ref, b2_ref)
    mix = _mix_stages(store(x1_ref, (s + 1) % 2), nxt % n_tiles, n_tiles, seq, alpha,
                      u_ref, uprev_ref, unext_ref, attn_ref, x_ref, *mix_w)
    for stage in MIXFFN_ORDER:
        next(mix if stage == "m" else ffn)
    assert next(mix, None) is None and next(ffn, None) is None


def _full(shape):
    return pl.BlockSpec(shape, lambda *_: (0,) * len(shape))


def _params(sem):
    return pltpu.CompilerParams(dimension_semantics=sem, vmem_limit_bytes=VMEM_LIMIT)


def _pad_heads(w, dim):
    r = w.shape[0]
    w = w.reshape(r, N_HEADS, dim)
    return jnp.pad(w, ((0, 0), (0, 0), (0, HEAD_SLOT - dim))).reshape(r, N_HEADS * HEAD_SLOT)


def _rope_slot(w):
    return jnp.concatenate([w, w[..., :ROPE_HALF], jnp.zeros_like(w[..., :ROPE_HALF])], axis=-1)


def _layer(x, cos_t, sin_t, w_in, pool_w, pool_scale, q_norm_g, w_q_up, kv_norm_g, w_k_up, w_v_up,
           w_o, ln1_g, ln1_b, w_gate, w_up, w_down, ln2_g, ln2_b, alpha):
    bsz, seq, _ = x.shape
    tm = 512
    tq = 512
    o_kr = POOL_WIDTH + Q_LORA_RANK + KV_LORA_RANK

    w_in_ext = jnp.concatenate(
        [w_in[:, :o_kr], jnp.zeros((D_MODEL, ROPE_LO), w_in.dtype), _rope_slot(w_in[:, o_kr:])],
        axis=1).astype(BF16)
    wq = _pad_heads(w_q_up, QK_HEAD_DIM).T.astype(BF16)
    wk = _pad_heads(w_k_up, QK_NOPE_DIM).astype(BF16)
    wv = jnp.pad(w_v_up.T.reshape(N_HEADS, V_HEAD_DIM, KV_LORA_RANK),
                 ((0, 0), (0, V_SLOT - V_HEAD_DIM), (0, 0))).reshape(VT_ROWS, KV_LORA_RANK).astype(BF16)
    ones_col = (jnp.arange(VT_ROWS) % V_SLOT == V_HEAD_DIM).astype(F32)[:, None]
    q_scale = (QK_HEAD_DIM ** -0.5) * math.log2(math.e)

    tok = lambda w: pl.BlockSpec((1, tm, w), lambda b, i: (b, i, 0))
    rope = lambda: pl.BlockSpec((ROPE_HALF, tm), lambda b, i: (0, b * (seq // tm) + i))
    u, q, k, v = pl.pallas_call(
        functools.partial(_proj_kernel, q_scale=q_scale),
        grid=(bsz, seq // tm),
        in_specs=[tok(D_MODEL), rope(), rope(), _full((D_MODEL, H_WIDTH)),
                  _full((1, Q_LORA_RANK)), _full((N_HEADS * HEAD_SLOT, Q_LORA_RANK)),
                  _full((1, KV_LORA_RANK)), _full((KV_LORA_RANK, N_HEADS * HEAD_SLOT)),
                  _full((VT_ROWS, KV_LORA_RANK)), _full((VT_ROWS, 1))],
        out_specs=[tok(POOL_WIDTH),
                   pl.BlockSpec((1, N_HEADS * HEAD_SLOT, tm), lambda b, i: (b, 0, i)),
                   tok(N_HEADS * HEAD_SLOT),
                   pl.BlockSpec((1, VT_ROWS, tm), lambda b, i: (b, 0, i))],
        out_shape=[jax.ShapeDtypeStruct((bsz, seq, POOL_WIDTH), F32),
                   jax.ShapeDtypeStruct((bsz, N_HEADS * HEAD_SLOT, seq), BF16),
                   jax.ShapeDtypeStruct((bsz, seq, N_HEADS * HEAD_SLOT), BF16),
                   jax.ShapeDtypeStruct((bsz, VT_ROWS, seq), BF16)],
        compiler_params=_params(("parallel", "parallel")),
        name="proj",
    )(x, cos_t, sin_t, w_in_ext, q_norm_g[None, :], wq, kv_norm_g[None, :], wk, wv, ones_col)

    hp = 8
    attn = pl.pallas_call(
        functools.partial(_attn_kernel, kc=256),
        grid=(bsz, N_HEADS // hp, seq // tq),
        in_specs=[pl.BlockSpec((1, hp * HEAD_SLOT, tq), lambda b, g, i: (b, g, i)),
                  pl.BlockSpec((1, hp * HEAD_SLOT, tq),
                               lambda b, g, i: (b, g, jnp.minimum(i + 1, seq // tq - 1))),
                  pl.BlockSpec((1, seq, hp * HEAD_SLOT), lambda b, g, i: (b, 0, g)),
                  pl.BlockSpec((1, hp * V_SLOT, seq), lambda b, g, i: (b, g, 0)),
                  _full((SUBLANES, tq))],
        out_specs=pl.BlockSpec((1, tq, hp * V_HEAD_DIM), lambda b, g, i: (b, i, g)),
        out_shape=jax.ShapeDtypeStruct((bsz, seq, ATTN_WIDTH), BF16),
        scratch_shapes=[pltpu.VMEM((2, seq, tq), F32),
                        pltpu.VMEM((SUBLANES, tq), F32)],
        compiler_params=_params(("parallel", "parallel", "arbitrary")),
        name="attn",
    )(q, q, k, v, jnp.zeros((SUBLANES, tq), jnp.int32))

    tf = 256
    rows = bsz * seq
    n_steps = rows // tf
    halo_blocks = tf // POOL_HALO
    last_halo = rows // POOL_HALO - 1
    ahead = lambda s: jnp.minimum(s + 1, n_steps - 1)
    row_blk = lambda w, idx, **kw: pl.BlockSpec((tf, w), lambda s: (idx(s), 0), **kw)
    halo = lambda idx, **kw: pl.BlockSpec((POOL_HALO, POOL_WIDTH), lambda s: (idx(s), 0), **kw)
    first = lambda s: 0
    single = dict(pipeline_mode=pl.Buffered(1))
    once = lambda shape: pl.BlockSpec(shape, lambda s: (0,) * len(shape), pipeline_mode=pl.Buffered(1))
    u2 = u.reshape(rows, POOL_WIDTH)
    attn2 = attn.reshape(rows, ATTN_WIDTH)
    xr = x.reshape(rows, D_MODEL)
    x2 = pl.pallas_call(
        functools.partial(_mixffn_kernel, seq=seq, alpha=alpha),
        grid=(n_steps,),
        in_specs=[row_blk(POOL_WIDTH, ahead),
                  halo(lambda s: jnp.maximum(ahead(s) * halo_blocks - 1, 0)),
                  halo(lambda s: jnp.minimum((ahead(s) + 1) * halo_blocks, last_halo)),
                  row_blk(ATTN_WIDTH, ahead), row_blk(D_MODEL, ahead),
                  row_blk(POOL_WIDTH, first, **single), halo(lambda s: halo_blocks, **single),
                  row_blk(ATTN_WIDTH, first, **single), row_blk(D_MODEL, first, **single),
                  once((len(POOL_WINDOWS), POOL_GROUP_DIM, POOL_GROUP_DIM)), once((1, POOL_WIDTH)),
                  once((D_MODEL, D_MODEL)), once((1, D_MODEL)), once((1, D_MODEL)),
                  once((D_MODEL, FFN_HIDDEN)), once((D_MODEL, FFN_HIDDEN)), once((FFN_HIDDEN, D_MODEL)),
                  once((1, D_MODEL)), once((1, D_MODEL))],
        out_specs=pl.BlockSpec((tf, D_MODEL), lambda s: (s, 0)),
        out_shape=jax.ShapeDtypeStruct((rows, D_MODEL), F32),
        scratch_shapes=[pltpu.VMEM((tf + 2 * POOL_HALO, POOL_WIDTH), F32),
                        pltpu.VMEM((2, tf, D_MODEL), F32)],
        compiler_params=_params(("arbitrary",)),
        name="mixffn",
    )(u2, u2, u2, attn2, xr, u2, u2, attn2, xr,
      pool_w.astype(BF16), pool_scale[None, :], w_o.astype(BF16), ln1_g[None, :], ln1_b[None, :],
      w_gate.astype(BF16), w_up.astype(BF16), w_down.astype(BF16), ln2_g[None, :], ln2_b[None, :])
    return x2.reshape(bsz, seq, D_MODEL)


def kernel(x, positions, w_in, pool_w, pool_scale, q_norm_g, w_q_up, kv_norm_g, w_k_up, w_v_up, w_o, ln1_g, ln1_b, w_gate, w_up, w_down, ln2_g, ln2_b):
    depth = w_in.shape[0]
    alpha = (2.0 * depth) ** 0.25
    inv_freq = 1.0 / (ROPE_THETA ** (jnp.arange(0, QK_ROPE_DIM, 2, dtype=F32) / QK_ROPE_DIM))
    tokens = positions.size
    plane = (ROPE_HALF, tokens // LANES, LANES)
    cos_t, sin_t = pl.pallas_call(
        _rope_table_kernel,
        in_specs=[pl.BlockSpec(memory_space=pltpu.SMEM), _full(plane[1:])],
        out_specs=[_full(plane), _full(plane)],
        out_shape=[jax.ShapeDtypeStruct(plane, F32)] * 2,
        grid=(1,),
        name="rope_table",
    )(inv_freq, positions.reshape(plane[1:]))
    cos_t = cos_t.reshape(ROPE_HALF, tokens)
    sin_t = sin_t.reshape(ROPE_HALF, tokens)
    for l in range(depth):
        x = _layer(x, cos_t, sin_t, w_in[l], pool_w[l], pool_scale[l], q_norm_g[l], w_q_up[l],
                   kv_norm_g[l], w_k_up[l], w_v_up[l], w_o[l], ln1_g[l], ln1_b[l],
                   w_gate[l], w_up[l], w_down[l], ln2_g[l], ln2_b[l], alpha)
    return x
```

```python
import functools
import itertools
import math

import jax
import jax.numpy as jnp
from jax import lax
from jax.experimental import pallas as pl
from jax.experimental.pallas import tpu as pltpu

D_MODEL = 1024
POOL_WIDTH = 512
POOL_WINDOWS = (2, 4, 8, 16)
POOL_GROUP_DIM = 128
N_HEADS = 8
QK_NOPE_DIM = 64
QK_ROPE_DIM = 32
V_HEAD_DIM = 64
QK_HEAD_DIM = QK_NOPE_DIM + QK_ROPE_DIM
ATTN_WIDTH = N_HEADS * V_HEAD_DIM
Q_LORA_RANK = 384
KV_LORA_RANK = 256
ROPE_THETA = 10000.0
FFN_HIDDEN = 2816
LN_EPS = 1e-5
RMS_EPS = 1e-6

LANES = 128
SUBLANES = 8
BF16_ROWS = 2 * SUBLANES
V_SLOT = V_HEAD_DIM + BF16_ROWS
VT_ROWS = N_HEADS * V_SLOT
HEAD_SLOT = LANES
ROPE_LO = QK_NOPE_DIM
ROPE_HALF = QK_ROPE_DIM // 2
H_WIDTH = POOL_WIDTH + Q_LORA_RANK + KV_LORA_RANK + LANES
POOL_HALO = 8
VMEM_LIMIT = 56 * 1024 * 1024

BF16 = jnp.bfloat16
F32 = jnp.float32


def _rope(t, cos, sin_signed):
    partner = pltpu.roll(t, LANES - ROPE_HALF, 1)
    return t * cos + partner * sin_signed


def _rms(t, g):
    y = t * lax.rsqrt(jnp.mean(t * t, axis=-1, keepdims=True) + RMS_EPS)
    return y * g


def _layer_norm(y, g, b):
    mu = jnp.mean(y, axis=-1, keepdims=True)
    d = y - mu
    var = jnp.mean(d * d, axis=-1, keepdims=True)
    return d * lax.rsqrt(var + LN_EPS) * g + b


def _rope_table_kernel(invf_ref, pos_ref, cos_ref, sin_ref):
    pos = pos_ref[...].astype(F32)
    for f in range(ROPE_HALF):
        ang = pos * invf_ref[f]
        cos_ref[f] = jnp.cos(ang)
        sin_ref[f] = jnp.sin(ang)


def _rope_lanes(t, outside):
    tm = t.shape[1]
    full = jnp.concatenate([jnp.full((ROPE_LO, tm), outside, F32), t, t,
                            jnp.zeros((LANES - ROPE_LO - QK_ROPE_DIM, tm), F32)], axis=0)
    return full.T


def _proj_kernel(x_ref, cos_ref, sin_ref, w_in_ref, qg_ref, wq_ref, kvg_ref, wk_ref, wv_ref, ones_ref,
                 u_ref, q_ref, k_ref, v_ref, *, q_scale):
    x = x_ref[0].astype(BF16)
    h = jnp.dot(x, w_in_ref[...], preferred_element_type=F32)
    u_ref[0] = h[:, :POOL_WIDTH]
    o_kv = POOL_WIDTH + Q_LORA_RANK
    o_kr = o_kv + KV_LORA_RANK
    cq = _rms(h[:, POOL_WIDTH:o_kv], qg_ref[...])
    ckv = _rms(h[:, o_kv:o_kr], kvg_ref[...])
    kr = h[:, o_kr:]

    cos_t = cos_ref[...]
    sin_t = sin_ref[...]
    cos = _rope_lanes(cos_t, 1.0)
    sin = _rope_lanes(sin_t, 0.0)
    first_half = lax.broadcasted_iota(jnp.int32, cos.shape, 1) < ROPE_LO + ROPE_HALF
    sin = jnp.where(first_half, -sin, sin)

    cq_b = cq.astype(BF16)
    ckv_b = ckv.astype(BF16)
    nt_dims = (((1,), (1,)), ((), ()))
    qt = lax.dot_general(wq_ref[...], cq_b, nt_dims, preferred_element_type=F32)
    k_nope = jnp.dot(ckv_b, wk_ref[...], preferred_element_type=F32)
    vt = lax.dot_general(wv_ref[...], ckv_b, nt_dims, preferred_element_type=F32)
    v_ref[0] = (vt + ones_ref[...]).astype(BF16)

    cos_q = cos_t * q_scale
    sin_q = sin_t * q_scale
    k_rope = _rope(kr, cos, sin)
    for hd in range(N_HEADS):
        sl = slice(hd * HEAD_SLOT, (hd + 1) * HEAD_SLOT)
        k_ref[0, :, sl] = (k_nope[:, sl] + k_rope).astype(BF16)
        base = hd * HEAD_SLOT
        r1 = qt[base + ROPE_LO:base + ROPE_LO + ROPE_HALF]
        r2 = qt[base + ROPE_LO + ROPE_HALF:base + ROPE_LO + QK_ROPE_DIM]
        q_ref[0, base:base + ROPE_LO, :] = (qt[base:base + ROPE_LO] * q_scale).astype(BF16)
        q_ref[0, base + ROPE_LO:base + HEAD_SLOT, :] = jnp.concatenate(
            [r1 * cos_q - r2 * sin_q, r2 * cos_q + r1 * sin_q,
             jnp.zeros((HEAD_SLOT - QK_HEAD_DIM, qt.shape[1]), F32)], axis=0).astype(BF16)


def _row_groups(t, op):
    return op(t.reshape(t.shape[0] // SUBLANES, SUBLANES, t.shape[1]), axis=0)


SKEW = 3


def _attn_kernel(q_ref, qn_ref, k_ref, vt_ref, zero_ref, w32_a, w32_b, w32_c,
                 o_ref, w16_a, w16_b, w16_c, s_ref, mx_ref, *, kc):
    seq = k_ref.shape[1]
    hp = k_ref.shape[2] // HEAD_SLOT
    n_chunks = seq // kc
    assert hp % 2 == 0

    def score_chunk(q, h, c):
        hs = slice(h * HEAD_SLOT, (h + 1) * HEAD_SLOT)
        ks = slice(c * kc, (c + 1) * kc)
        s = jnp.dot(k_ref[0, ks, hs], q[0, hs, :], preferred_element_type=F32)
        s_ref[h % 2, ks, :] = s
        return _row_groups(s, jnp.max)

    def fold(cms):
        mx = cms[0]
        for cm in cms[1:]:
            mx = jnp.maximum(mx, cm)
        return mx

    @pl.when(pl.program_id(2) == 0)
    def _():
        mx_ref[...] = fold([score_chunk(q_ref, 0, c) for c in range(n_chunks)])

    mx = mx_ref[...]
    zero_bits = zero_ref[0:1, :]
    outs = []
    for h in range(hp):
        m = jnp.max(mx, axis=0, keepdims=True)
        ds = slice(h * V_SLOT, (h + 1) * V_SLOT)
        cms = []
        acc = None
        for c in range(n_chunks):
            ks = slice(c * kc, (c + 1) * kc)
            cms.append(score_chunk(q_ref, h + 1, c) if h + 1 < hp else score_chunk(qn_ref, 0, c))
            m_c = m
            if c >= SKEW:
                dep = lax.bitcast_convert_type(cms[c - SKEW][0:1, :], jnp.int32) & zero_bits
                m_c = m + lax.bitcast_convert_type(dep, F32)
            p = jnp.exp2(s_ref[h % 2, ks, :] - m_c)
            o = jnp.dot(vt_ref[0, ds, ks], p.astype(BF16), preferred_element_type=F32)
            acc = o if acc is None else acc + o
        outs.append(acc[:V_HEAD_DIM] / acc[V_HEAD_DIM:V_HEAD_DIM + 1])
        mx = fold(cms)
    mx_ref[...] = mx
    o_ref[0] = jnp.concatenate(outs, axis=0).T.astype(BF16)
    for w32, w16 in ((w32_a, w16_a), (w32_b, w16_b), (w32_c, w16_c)):
        w16[...] = w32[...].astype(BF16)


def _mix_stages(emit, i, n_tiles, seq, alpha, u_ref, uprev_ref, unext_ref, attn_ref, x_ref,
                pw_ref, ps_ref, wo_ref, g_ref, b_ref, ext_ref):
    tm = u_ref.shape[0]
    ext_ref[0:POOL_HALO] = jnp.where(i > 0, uprev_ref[...], 0.0)
    ext_ref[POOL_HALO:POOL_HALO + tm] = u_ref[...]
    ext_ref[POOL_HALO + tm:] = jnp.where(i < n_tiles - 1, unext_ref[...], 0.0)
    row = i * tm + lax.broadcasted_iota(jnp.int32, (tm, 1), 0)
    mixed = []
    for g, w in enumerate(POOL_WINDOWS):
        cols = slice(g * POOL_GROUP_DIM, (g + 1) * POOL_GROUP_DIM)
        back = w // 2
        win = ext_ref[pl.ds(POOL_HALO - back, tm), cols]
        for d in range(1 - back, w - back):
            win = win + ext_ref[pl.ds(POOL_HALO + d, tm), cols]
        cnt = (jnp.minimum(row + (w - back), seq) - jnp.maximum(row - back, 0)).astype(F32)
        pg = win / cnt - ext_ref[pl.ds(POOL_HALO, tm), cols]
        og = jnp.dot(pg.astype(BF16), pw_ref[g], preferred_element_type=F32) * ps_ref[:, cols]
        mixed.append(og.astype(BF16))
        yield
    mixed.append(attn_ref[...])
    acc = jnp.dot(jnp.concatenate(mixed, axis=1), wo_ref[...], preferred_element_type=F32)
    y = alpha * x_ref[...] + acc
    emit(_layer_norm(y, g_ref[...], b_ref[...]))
    yield


FFN_CHUNKS = ((0, 768), (768, 1536), (1536, 2304), (2304, FFN_HIDDEN))


def _ffn_stages(emit, x, alpha, wg_ref, wu_ref, wd_ref, g_ref, b_ref):
    xb = x.astype(BF16)
    ffn = None
    for lo, hi in FFN_CHUNKS:
        gate = jnp.dot(xb, wg_ref[:, lo:hi], preferred_element_type=F32)
        up = jnp.dot(xb, wu_ref[:, lo:hi], preferred_element_type=F32)
        hid = gate * (1.0 / (1.0 + jnp.exp(-gate))) * up
        part = jnp.dot(hid.astype(BF16), wd_ref[lo:hi, :], preferred_element_type=F32)
        ffn = part if ffn is None else ffn + part
        yield
    y = alpha * x + ffn
    emit(_layer_norm(y, g_ref[...], b_ref[...]))
    yield


MIXFFN_ORDER = "fmfmfmfmfm"


def _mixffn_kernel(u_ref, uprev_ref, unext_ref, attn_ref, x_ref,
                   u0_ref, unext0_ref, attn0_ref, x0_ref,
                   pw_ref, ps_ref, wo_ref, g1_ref, b1_ref, wg_ref, wu_ref, wd_ref, g2_ref, b2_ref,
                   out_ref, ext_ref, x1_ref, *, seq, alpha):
    s = pl.program_id(0)
    n_steps = pl.num_programs(0)
    tm = u_ref.shape[0]
    n_tiles = seq // tm
    mix_w = (pw_ref, ps_ref, wo_ref, g1_ref, b1_ref, ext_ref)

    def store(ref, idx):
        def emit(v):
            ref[idx] = v
        return emit

    @pl.when(s == 0)
    def _():
        for _ in _mix_stages(store(x1_ref, 0), 0, n_tiles, seq, alpha, u0_ref, unext0_ref, unext0_ref,
                             attn0_ref, x0_ref, *mix_w):
            pass

    nxt = jnp.minimum(s + 1, n_steps - 1)
    ffn = _ffn_stages(store(out_ref, ...), x1_ref[s % 2], alpha, wg_ref, wu_ref, wd_ref, g2_ref, b2_ref)
    mix = _mix_stages(store(x1_ref, (s + 1) % 2), nxt % n_tiles, n_tiles, seq, alpha,
                      u_ref, uprev_ref, unext_ref, attn_ref, x_ref, *mix_w)
    for stage in MIXFFN_ORDER:
        next(mix if stage == "m" else ffn)
    assert next(mix, None) is None and next(ffn, None) is None


def _full(shape):
    return pl.BlockSpec(shape, lambda *_: (0,) * len(shape))


def _params(sem):
    return pltpu.CompilerParams(dimension_semantics=sem, vmem_limit_bytes=VMEM_LIMIT)


def _pad_heads(w, dim):
    r = w.shape[0]
    w = w.reshape(r, N_HEADS, dim)
    return jnp.pad(w, ((0, 0), (0, 0), (0, HEAD_SLOT - dim))).reshape(r, N_HEADS * HEAD_SLOT)


def _rope_slot(w):
    return jnp.concatenate([w, w[..., :ROPE_HALF], jnp.zeros_like(w[..., :ROPE_HALF])], axis=-1)


def _layer(x, cos_t, sin_t, w_in, pool_w, pool_scale, q_norm_g, w_q_up, kv_norm_g, w_k_up, w_v_up,
           w_o, ln1_g, ln1_b, w_gate, w_up, w_down, ln2_g, ln2_b, alpha):
    bsz, seq, _ = x.shape
    tm = 512
    tq = 512
    o_kr = POOL_WIDTH + Q_LORA_RANK + KV_LORA_RANK

    w_in_ext = jnp.concatenate(
        [w_in[:, :o_kr], jnp.zeros((D_MODEL, ROPE_LO), w_in.dtype), _rope_slot(w_in[:, o_kr:])],
        axis=1).astype(BF16)
    wq = _pad_heads(w_q_up, QK_HEAD_DIM).T.astype(BF16)
    wk = _pad_heads(w_k_up, QK_NOPE_DIM).astype(BF16)
    wv = jnp.pad(w_v_up.T.reshape(N_HEADS, V_HEAD_DIM, KV_LORA_RANK),
                 ((0, 0), (0, V_SLOT - V_HEAD_DIM), (0, 0))).reshape(VT_ROWS, KV_LORA_RANK).astype(BF16)
    ones_col = (jnp.arange(VT_ROWS) % V_SLOT == V_HEAD_DIM).astype(F32)[:, None]
    q_scale = (QK_HEAD_DIM ** -0.5) * math.log2(math.e)

    tok = lambda w: pl.BlockSpec((1, tm, w), lambda b, i: (b, i, 0))
    rope = lambda: pl.BlockSpec((ROPE_HALF, tm), lambda b, i: (0, b * (seq // tm) + i))
    u, q, k, v = pl.pallas_call(
        functools.partial(_proj_kernel, q_scale=q_scale),
        grid=(bsz, seq // tm),
        in_specs=[tok(D_MODEL), rope(), rope(), _full((D_MODEL, H_WIDTH)),
                  _full((1, Q_LORA_RANK)), _full((N_HEADS * HEAD_SLOT, Q_LORA_RANK)),
                  _full((1, KV_LORA_RANK)), _full((KV_LORA_RANK, N_HEADS * HEAD_SLOT)),
                  _full((VT_ROWS, KV_LORA_RANK)), _full((VT_ROWS, 1))],
        out_specs=[tok(POOL_WIDTH),
                   pl.BlockSpec((1, N_HEADS * HEAD_SLOT, tm), lambda b, i: (b, 0, i)),
                   tok(N_HEADS * HEAD_SLOT),
                   pl.BlockSpec((1, VT_ROWS, tm), lambda b, i: (b, 0, i))],
        out_shape=[jax.ShapeDtypeStruct((bsz, seq, POOL_WIDTH), F32),
                   jax.ShapeDtypeStruct((bsz, N_HEADS * HEAD_SLOT, seq), BF16),
                   jax.ShapeDtypeStruct((bsz, seq, N_HEADS * HEAD_SLOT), BF16),
                   jax.ShapeDtypeStruct((bsz, VT_ROWS, seq), BF16)],
        compiler_params=_params(("parallel", "parallel")),
        name="proj",
    )(x, cos_t, sin_t, w_in_ext, q_norm_g[None, :], wq, kv_norm_g[None, :], wk, wv, ones_col)

    hp = 8
    n_groups, n_q = N_HEADS // hp, seq // tq
    steps = bsz * n_groups * n_q
    w_rows, wd_rows = D_MODEL // steps, FFN_HIDDEN // (steps // 2)
    assert w_rows % BF16_ROWS == 0 and wd_rows % BF16_ROWS == 0
    step_of = lambda b, g, i: (b * n_groups + g) * n_q + i
    w_slice = pl.BlockSpec((w_rows, FFN_HIDDEN), lambda b, g, i: (step_of(b, g, i), 0))
    wd_slice = pl.BlockSpec((wd_rows, D_MODEL), lambda b, g, i: (step_of(b, g, i) // 2, 0))
    attn, wg16, wu16, wd16 = pl.pallas_call(
        functools.partial(_attn_kernel, kc=256),
        grid=(bsz, n_groups, n_q),
        in_specs=[pl.BlockSpec((1, hp * HEAD_SLOT, tq), lambda b, g, i: (b, g, i)),
                  pl.BlockSpec((1, hp * HEAD_SLOT, tq),
                               lambda b, g, i: (b, g, jnp.minimum(i + 1, n_q - 1))),
                  pl.BlockSpec((1, seq, hp * HEAD_SLOT), lambda b, g, i: (b, 0, g)),
                  pl.BlockSpec((1, hp * V_SLOT, seq), lambda b, g, i: (b, g, 0)),
                  _full((SUBLANES, tq)), w_slice, w_slice, wd_slice],
        out_specs=[pl.BlockSpec((1, tq, hp * V_HEAD_DIM), lambda b, g, i: (b, i, g)),
                   w_slice, w_slice, wd_slice],
        out_shape=[jax.ShapeDtypeStruct((bsz, seq, ATTN_WIDTH), BF16)]
                  + [jax.ShapeDtypeStruct(w.shape, BF16) for w in (w_gate, w_up, w_down)],
        scratch_shapes=[pltpu.VMEM((2, seq, tq), F32),
                        pltpu.VMEM((SUBLANES, tq), F32)],
        compiler_params=_params(("arbitrary", "arbitrary", "arbitrary")),
        name="attn",
    )(q, q, k, v, jnp.zeros((SUBLANES, tq), jnp.int32), w_gate, w_up, w_down)

    tf = 256
    rows = bsz * seq
    n_steps = rows // tf
    halo_blocks = tf // POOL_HALO
    last_halo = rows // POOL_HALO - 1
    ahead = lambda s: jnp.minimum(s + 1, n_steps - 1)
    row_blk = lambda w, idx, **kw: pl.BlockSpec((tf, w), lambda s: (idx(s), 0), **kw)
    halo = lambda idx, **kw: pl.BlockSpec((POOL_HALO, POOL_WIDTH), lambda s: (idx(s), 0), **kw)
    first = lambda s: 0
    single = dict(pipeline_mode=pl.Buffered(1))
    once = lambda shape: pl.BlockSpec(shape, lambda s: (0,) * len(shape), pipeline_mode=pl.Buffered(1))
    u2 = u.reshape(rows, POOL_WIDTH)
    attn2 = attn.reshape(rows, ATTN_WIDTH)
    xr = x.reshape(rows, D_MODEL)
    x2 = pl.pallas_call(
        functools.partial(_mixffn_kernel, seq=seq, alpha=alpha),
        grid=(n_steps,),
        in_specs=[row_blk(POOL_WIDTH, ahead),
                  halo(lambda s: jnp.maximum(ahead(s) * halo_blocks - 1, 0)),
                  halo(lambda s: jnp.minimum((ahead(s) + 1) * halo_blocks, last_halo)),
                  row_blk(ATTN_WIDTH, ahead), row_blk(D_MODEL, ahead),
                  row_blk(POOL_WIDTH, first, **single), halo(lambda s: halo_blocks, **single),
                  row_blk(ATTN_WIDTH, first, **single), row_blk(D_MODEL, first, **single),
                  once((len(POOL_WINDOWS), POOL_GROUP_DIM, POOL_GROUP_DIM)), once((1, POOL_WIDTH)),
                  once((D_MODEL, D_MODEL)), once((1, D_MODEL)), once((1, D_MODEL)),
                  once((D_MODEL, FFN_HIDDEN)), once((D_MODEL, FFN_HIDDEN)), once((FFN_HIDDEN, D_MODEL)),
                  once((1, D_MODEL)), once((1, D_MODEL))],
        out_specs=pl.BlockSpec((tf, D_MODEL), lambda s: (s, 0)),
        out_shape=jax.ShapeDtypeStruct((rows, D_MODEL), F32),
        scratch_shapes=[pltpu.VMEM((tf + 2 * POOL_HALO, POOL_WIDTH), F32),
                        pltpu.VMEM((2, tf, D_MODEL), F32)],
        compiler_params=_params(("arbitrary",)),
        name="mixffn",
    )(u2, u2, u2, attn2, xr, u2, u2, attn2, xr,
      pool_w.astype(BF16), pool_scale[None, :], w_o.astype(BF16), ln1_g[None, :], ln1_b[None, :],
      wg16, wu16, wd16, ln2_g[None, :], ln2_b[None, :])
    return x2.reshape(bsz, seq, D_MODEL)


def kernel(x, positions, w_in, pool_w, pool_scale, q_norm_g, w_q_up, kv_norm_g, w_k_up, w_v_up, w_o, ln1_g, ln1_b, w_gate, w_up, w_down, ln2_g, ln2_b):
    depth = w_in.shape[0]
    alpha = (2.0 * depth) ** 0.25
    inv_freq = 1.0 / (ROPE_THETA ** (jnp.arange(0, QK_ROPE_DIM, 2, dtype=F32) / QK_ROPE_DIM))
    tokens = positions.size
    plane = (ROPE_HALF, tokens // LANES, LANES)
    cos_t, sin_t = pl.pallas_call(
        _rope_table_kernel,
        in_specs=[pl.BlockSpec(memory_space=pltpu.SMEM), _full(plane[1:])],
        out_specs=[_full(plane), _full(plane)],
        out_shape=[jax.ShapeDtypeStruct(plane, F32)] * 2,
        grid=(1,),
        name="rope_table",
    )(inv_freq, positions.reshape(plane[1:]))
    cos_t = cos_t.reshape(ROPE_HALF, tokens)
    sin_t = sin_t.reshape(ROPE_HALF, tokens)
    for l in range(depth):
        x = _layer(x, cos_t, sin_t, w_in[l], pool_w[l], pool_scale[l], q_norm_g[l], w_q_up[l],
                   kv_norm_g[l], w_k_up[l], w_v_up[l], w_o[l], ln1_g[l], ln1_b[l],
                   w_gate[l], w_up[l], w_down[l], ln2_g[l], ln2_b[l], alpha)
    return x
```

```python
import functools
import math

import jax
import jax.numpy as jnp
from jax import lax
from jax.experimental import pallas as pl
from jax.experimental.pallas import tpu as pltpu

D_MODEL = 1024
POOL_WIDTH = 512
POOL_WINDOWS = (2, 4, 8, 16)
POOL_GROUP_DIM = 128
N_HEADS = 8
QK_NOPE_DIM = 64
QK_ROPE_DIM = 32
V_HEAD_DIM = 64
QK_HEAD_DIM = QK_NOPE_DIM + QK_ROPE_DIM
ATTN_WIDTH = N_HEADS * V_HEAD_DIM
Q_LORA_RANK = 384
KV_LORA_RANK = 256
ROPE_THETA = 10000.0
FFN_HIDDEN = 2816
LN_EPS = 1e-5
RMS_EPS = 1e-6

LANES = 128
SUBLANES = 8
BF16_ROWS = 2 * SUBLANES
MXU_COLS = 256
V_SLOT = V_HEAD_DIM + BF16_ROWS
VT_ROWS = N_HEADS * V_SLOT
HEAD_SLOT = LANES
ROPE_LO = QK_NOPE_DIM
ROPE_HALF = QK_ROPE_DIM // 2
H_WIDTH = POOL_WIDTH + Q_LORA_RANK + KV_LORA_RANK + LANES
POOL_HALO = 8
VMEM_LIMIT = 56 * 1024 * 1024

PROJ_TILE = 512
ATTN_Q_TILE = 512
ATTN_KEY_CHUNK = 256
ATTN_HEADS_PER_STEP = 8
ATTN_SKEW = 3
FFN_TILE = 256
FFN_CHUNKS = ((0, 768), (768, 1536), (1536, 2304), (2304, FFN_HIDDEN))
assert all(lo % MXU_COLS == 0 for lo, _ in FFN_CHUNKS)

BF16 = jnp.bfloat16
F32 = jnp.float32


def _rope(t, cos, sin_signed):
    partner = pltpu.roll(t, LANES - ROPE_HALF, 1)
    return t * cos + partner * sin_signed


def _rms(t, g):
    y = t * lax.rsqrt(jnp.mean(t * t, axis=-1, keepdims=True) + RMS_EPS)
    return y * g


def _layer_norm(y, g, b):
    mu = jnp.mean(y, axis=-1, keepdims=True)
    d = y - mu
    var = jnp.mean(d * d, axis=-1, keepdims=True)
    return d * lax.rsqrt(var + LN_EPS) * g + b


def _rope_table_kernel(invf_ref, pos_ref, cos_ref, sin_ref):
    pos = pos_ref[...].astype(F32)
    for f in range(ROPE_HALF):
        ang = pos * invf_ref[f]
        cos_ref[f] = jnp.cos(ang)
        sin_ref[f] = jnp.sin(ang)


def _rope_lanes(t, outside):
    tm = t.shape[1]
    full = jnp.concatenate([jnp.full((ROPE_LO, tm), outside, F32), t, t,
                            jnp.zeros((LANES - ROPE_LO - QK_ROPE_DIM, tm), F32)], axis=0)
    return full.T


def _proj_kernel(x_ref, cos_ref, sin_ref, w_in_ref, qg_ref, wq_ref, kvg_ref, wk_ref, wv_ref, ones_ref,
                 u_ref, q_ref, k_ref, v_ref, *, q_scale):
    x = x_ref[0].astype(BF16)
    h = jnp.dot(x, w_in_ref[...], preferred_element_type=F32)
    u_ref[0] = h[:, :POOL_WIDTH]
    o_kv = POOL_WIDTH + Q_LORA_RANK
    o_kr = o_kv + KV_LORA_RANK
    cq = _rms(h[:, POOL_WIDTH:o_kv], qg_ref[...])
    ckv = _rms(h[:, o_kv:o_kr], kvg_ref[...])
    kr = h[:, o_kr:]

    cos_t = cos_ref[...]
    sin_t = sin_ref[...]
    cos = _rope_lanes(cos_t, 1.0)
    sin = _rope_lanes(sin_t, 0.0)
    first_half = lax.broadcasted_iota(jnp.int32, cos.shape, 1) < ROPE_LO + ROPE_HALF
    sin = jnp.where(first_half, -sin, sin)

    cq_b = cq.astype(BF16)
    ckv_b = ckv.astype(BF16)
    nt_dims = (((1,), (1,)), ((), ()))
    qt = lax.dot_general(wq_ref[...], cq_b, nt_dims, preferred_element_type=F32)
    k_nope = jnp.dot(ckv_b, wk_ref[...], preferred_element_type=F32)
    vt = lax.dot_general(wv_ref[...], ckv_b, nt_dims, preferred_element_type=F32)
    v_ref[0] = (vt + ones_ref[...]).astype(BF16)

    cos_q = cos_t * q_scale
    sin_q = sin_t * q_scale
    k_rope = _rope(kr, cos, sin)
    for hd in range(N_HEADS):
        sl = slice(hd * HEAD_SLOT, (hd + 1) * HEAD_SLOT)
        k_ref[0, :, sl] = (k_nope[:, sl] + k_rope).astype(BF16)
        base = hd * HEAD_SLOT
        r1 = qt[base + ROPE_LO:base + ROPE_LO + ROPE_HALF]
        r2 = qt[base + ROPE_LO + ROPE_HALF:base + ROPE_LO + QK_ROPE_DIM]
        q_ref[0, base:base + ROPE_LO, :] = (qt[base:base + ROPE_LO] * q_scale).astype(BF16)
        q_ref[0, base + ROPE_LO:base + HEAD_SLOT, :] = jnp.concatenate(
            [r1 * cos_q - r2 * sin_q, r2 * cos_q + r1 * sin_q,
             jnp.zeros((HEAD_SLOT - QK_HEAD_DIM, qt.shape[1]), F32)], axis=0).astype(BF16)


def _row_groups(t, op):
    return op(t.reshape(t.shape[0] // SUBLANES, SUBLANES, t.shape[1]), axis=0)


def _attn_kernel(q_ref, qn_ref, k_ref, vt_ref, zero_ref, w32_a, w32_b, w32_c,
                 o_ref, w16_a, w16_b, w16_c, s_ref, mx_ref, *, kc):
    seq = k_ref.shape[1]
    hp = k_ref.shape[2] // HEAD_SLOT
    n_chunks = seq // kc
    assert hp % 2 == 0

    def score_chunk(q, h, c):
        hs = slice(h * HEAD_SLOT, (h + 1) * HEAD_SLOT)
        ks = slice(c * kc, (c + 1) * kc)
        s = jnp.dot(k_ref[0, ks, hs], q[0, hs, :], preferred_element_type=F32)
        s_ref[h % 2, ks, :] = s
        return _row_groups(s, jnp.max)

    def fold(cms):
        mx = cms[0]
        for cm in cms[1:]:
            mx = jnp.maximum(mx, cm)
        return mx

    @pl.when(pl.program_id(2) == 0)
    def _():
        mx_ref[...] = fold([score_chunk(q_ref, 0, c) for c in range(n_chunks)])

    mx = mx_ref[...]
    zero_bits = zero_ref[0:1, :]
    outs = []
    for h in range(hp):
        m = jnp.max(mx, axis=0, keepdims=True)
        ds = slice(h * V_SLOT, (h + 1) * V_SLOT)
        cms = []
        acc = None
        for c in range(n_chunks):
            ks = slice(c * kc, (c + 1) * kc)
            cms.append(score_chunk(q_ref, h + 1, c) if h + 1 < hp else score_chunk(qn_ref, 0, c))
            m_c = m
            if c >= ATTN_SKEW:
                dep = lax.bitcast_convert_type(cms[c - ATTN_SKEW][0:1, :], jnp.int32) & zero_bits
                m_c = m + lax.bitcast_convert_type(dep, F32)
            p = jnp.exp2(s_ref[h % 2, ks, :] - m_c)
            o = jnp.dot(vt_ref[0, ds, ks], p.astype(BF16), preferred_element_type=F32)
            acc = o if acc is None else acc + o
        outs.append(acc[:V_HEAD_DIM] / acc[V_HEAD_DIM:V_HEAD_DIM + 1])
        mx = fold(cms)
    mx_ref[...] = mx
    o_ref[0] = jnp.concatenate(outs, axis=0).T.astype(BF16)
    for w32, w16 in ((w32_a, w16_a), (w32_b, w16_b), (w32_c, w16_c)):
        w16[...] = w32[...].astype(BF16)


def _mix_stages(emit, i, n_tiles, seq, alpha, u_ref, uprev_ref, unext_ref, attn_ref, x_ref,
                pw_ref, ps_ref, wo_ref, g_ref, b_ref, ext_ref):
    tm = u_ref.shape[0]
    ext_ref[0:POOL_HALO] = jnp.where(i > 0, uprev_ref[...], 0.0)
    ext_ref[POOL_HALO:POOL_HALO + tm] = u_ref[...]
    ext_ref[POOL_HALO + tm:] = jnp.where(i < n_tiles - 1, unext_ref[...], 0.0)
    row = i * tm + lax.broadcasted_iota(jnp.int32, (tm, 1), 0)
    mixed = []
    for g, w in enumerate(POOL_WINDOWS):
        cols = slice(g * POOL_GROUP_DIM, (g + 1) * POOL_GROUP_DIM)
        back = w // 2
        win = ext_ref[pl.ds(POOL_HALO - back, tm), cols]
        for d in range(1 - back, w - back):
            win = win + ext_ref[pl.ds(POOL_HALO + d, tm), cols]
        cnt = (jnp.minimum(row + (w - back), seq) - jnp.maximum(row - back, 0)).astype(F32)
        pg = win / cnt - ext_ref[pl.ds(POOL_HALO, tm), cols]
        og = jnp.dot(pg.astype(BF16), pw_ref[g], preferred_element_type=F32) * ps_ref[:, cols]
        mixed.append(og.astype(BF16))
        yield
    mixed.append(attn_ref[...])
    acc = jnp.dot(jnp.concatenate(mixed, axis=1), wo_ref[...], preferred_element_type=F32)
    y = alpha * x_ref[...] + acc
    emit(_layer_norm(y, g_ref[...], b_ref[...]))
    yield


def _ffn_stages(emit, x, alpha, wg_ref, wu_ref, wd_ref, g_ref, b_ref):
    xb = x.astype(BF16)
    ffn = None
    for lo, hi in FFN_CHUNKS:
        gate = jnp.dot(xb, wg_ref[:, lo:hi], preferred_element_type=F32)
        up = jnp.dot(xb, wu_ref[:, lo:hi], preferred_element_type=F32)
        hid = gate * (1.0 / (1.0 + jnp.exp(-gate))) * up
        part = jnp.dot(hid.astype(BF16), wd_ref[lo:hi, :], preferred_element_type=F32)
        ffn = part if ffn is None else ffn + part
        yield
    y = alpha * x + ffn
    emit(_layer_norm(y, g_ref[...], b_ref[...]))
    yield


def _mixffn_kernel(u_ref, uprev_ref, unext_ref, attn_ref, x_ref,
                   u0_ref, unext0_ref, attn0_ref, x0_ref,
                   pw_ref, ps_ref, wo_ref, g1_ref, b1_ref, wg_ref, wu_ref, wd_ref, g2_ref, b2_ref,
                   out_ref, ext_ref, x1_ref, *, seq, alpha):
    s = pl.program_id(0)
    n_steps = pl.num_programs(0)
    tm = u_ref.shape[0]
    n_tiles = seq // tm
    mix_w = (pw_ref, ps_ref, wo_ref, g1_ref, b1_ref, ext_ref)

    def store(ref, idx):
        def emit(v):
            ref[idx] = v
        return emit

    @pl.when(s == 0)
    def _():
        for _ in _mix_stages(store(x1_ref, 0), 0, n_tiles, seq, alpha, u0_ref, unext0_ref, unext0_ref,
                             attn0_ref, x0_ref, *mix_w):
            pass

    nxt = jnp.minimum(s + 1, n_steps - 1)
    ffn = _ffn_stages(store(out_ref, ...), x1_ref[s % 2], alpha, wg_ref, wu_ref, wd_ref, g2_ref, b2_ref)
    mix = _mix_stages(store(x1_ref, (s + 1) % 2), nxt % n_tiles, n_tiles, seq, alpha,
                      u_ref, uprev_ref, unext_ref, attn_ref, x_ref, *mix_w)
    for _ in range(len(FFN_CHUNKS) + 1):
        next(ffn)
        next(mix)
    assert next(mix, None) is None and next(ffn, None) is None


def _full(shape):
    return pl.BlockSpec(shape, lambda *_: (0,) * len(shape))


def _params(sem):
    return pltpu.CompilerParams(dimension_semantics=sem, vmem_limit_bytes=VMEM_LIMIT)


def _pad_heads(w, dim):
    r = w.shape[0]
    w = w.reshape(r, N_HEADS, dim)
    return jnp.pad(w, ((0, 0), (0, 0), (0, HEAD_SLOT - dim))).reshape(r, N_HEADS * HEAD_SLOT)


def _rope_slot(w):
    return jnp.concatenate([w, w[..., :ROPE_HALF], jnp.zeros_like(w[..., :ROPE_HALF])], axis=-1)


def _layer(x, cos_t, sin_t, w_in, pool_w, pool_scale, q_norm_g, w_q_up, kv_norm_g, w_k_up, w_v_up,
           w_o, ln1_g, ln1_b, w_gate, w_up, w_down, ln2_g, ln2_b, alpha):
    bsz, seq, _ = x.shape
    tm, tq, tf, hp = PROJ_TILE, ATTN_Q_TILE, FFN_TILE, ATTN_HEADS_PER_STEP
    assert seq % tm == 0 and seq % tq == 0 and seq % tf == 0 and N_HEADS % hp == 0
    assert len(POOL_WINDOWS) == len(FFN_CHUNKS)
    o_kr = POOL_WIDTH + Q_LORA_RANK + KV_LORA_RANK

    w_in_ext = jnp.concatenate(
        [w_in[:, :o_kr], jnp.zeros((D_MODEL, ROPE_LO), w_in.dtype), _rope_slot(w_in[:, o_kr:])],
        axis=1).astype(BF16)
    wq = _pad_heads(w_q_up, QK_HEAD_DIM).T.astype(BF16)
    wk = _pad_heads(w_k_up, QK_NOPE_DIM).astype(BF16)
    wv = jnp.pad(w_v_up.T.reshape(N_HEADS, V_HEAD_DIM, KV_LORA_RANK),
                 ((0, 0), (0, V_SLOT - V_HEAD_DIM), (0, 0))).reshape(VT_ROWS, KV_LORA_RANK).astype(BF16)
    ones_col = (jnp.arange(VT_ROWS) % V_SLOT == V_HEAD_DIM).astype(F32)[:, None]
    q_scale = (QK_HEAD_DIM ** -0.5) * math.log2(math.e)

    tok = lambda w: pl.BlockSpec((1, tm, w), lambda b, i: (b, i, 0))
    rope = lambda: pl.BlockSpec((ROPE_HALF, tm), lambda b, i: (0, b * (seq // tm) + i))
    u, q, k, v = pl.pallas_call(
        functools.partial(_proj_kernel, q_scale=q_scale),
        grid=(bsz, seq // tm),
        in_specs=[tok(D_MODEL), rope(), rope(), _full((D_MODEL, H_WIDTH)),
                  _full((1, Q_LORA_RANK)), _full((N_HEADS * HEAD_SLOT, Q_LORA_RANK)),
                  _full((1, KV_LORA_RANK)), _full((KV_LORA_RANK, N_HEADS * HEAD_SLOT)),
                  _full((VT_ROWS, KV_LORA_RANK)), _full((VT_ROWS, 1))],
        out_specs=[tok(POOL_WIDTH),
                   pl.BlockSpec((1, N_HEADS * HEAD_SLOT, tm), lambda b, i: (b, 0, i)),
                   tok(N_HEADS * HEAD_SLOT),
                   pl.BlockSpec((1, VT_ROWS, tm), lambda b, i: (b, 0, i))],
        out_shape=[jax.ShapeDtypeStruct((bsz, seq, POOL_WIDTH), F32),
                   jax.ShapeDtypeStruct((bsz, N_HEADS * HEAD_SLOT, seq), BF16),
                   jax.ShapeDtypeStruct((bsz, seq, N_HEADS * HEAD_SLOT), BF16),
                   jax.ShapeDtypeStruct((bsz, VT_ROWS, seq), BF16)],
        compiler_params=_params(("parallel", "parallel")),
        name="proj",
    )(x, cos_t, sin_t, w_in_ext, q_norm_g[None, :], wq, kv_norm_g[None, :], wk, wv, ones_col)

    n_groups, n_q = N_HEADS // hp, seq // tq
    steps = bsz * n_groups * n_q
    w_rows, wd_rows = D_MODEL // steps, FFN_HIDDEN // (steps // 2)
    assert w_rows * steps == D_MODEL and wd_rows * (steps // 2) == FFN_HIDDEN
    assert w_rows % BF16_ROWS == 0 and wd_rows % BF16_ROWS == 0
    step_of = lambda b, g, i: (b * n_groups + g) * n_q + i
    w_slice = pl.BlockSpec((w_rows, FFN_HIDDEN), lambda b, g, i: (step_of(b, g, i), 0))
    wd_slice = pl.BlockSpec((wd_rows, D_MODEL), lambda b, g, i: (step_of(b, g, i) // 2, 0))
    attn, wg16, wu16, wd16 = pl.pallas_call(
        functools.partial(_attn_kernel, kc=ATTN_KEY_CHUNK),
        grid=(bsz, n_groups, n_q),
        in_specs=[pl.BlockSpec((1, hp * HEAD_SLOT, tq), lambda b, g, i: (b, g, i)),
                  pl.BlockSpec((1, hp * HEAD_SLOT, tq),
                               lambda b, g, i: (b, g, jnp.minimum(i + 1, n_q - 1))),
                  pl.BlockSpec((1, seq, hp * HEAD_SLOT), lambda b, g, i: (b, 0, g)),
                  pl.BlockSpec((1, hp * V_SLOT, seq), lambda b, g, i: (b, g, 0)),
                  _full((SUBLANES, tq)), w_slice, w_slice, wd_slice],
        out_specs=[pl.BlockSpec((1, tq, hp * V_HEAD_DIM), lambda b, g, i: (b, i, g)),
                   w_slice, w_slice, wd_slice],
        out_shape=[jax.ShapeDtypeStruct((bsz, seq, ATTN_WIDTH), BF16)]
                  + [jax.ShapeDtypeStruct(w.shape, BF16) for w in (w_gate, w_up, w_down)],
        scratch_shapes=[pltpu.VMEM((2, seq, tq), F32),
                        pltpu.VMEM((SUBLANES, tq), F32)],
        compiler_params=_params(("arbitrary", "arbitrary", "arbitrary")),
        name="attn",
    )(q, q, k, v, jnp.zeros((SUBLANES, tq), jnp.int32), w_gate, w_up, w_down)

    rows = bsz * seq
    n_steps = rows // tf
    halo_blocks = tf // POOL_HALO
    last_halo = rows // POOL_HALO - 1
    ahead = lambda s: jnp.minimum(s + 1, n_steps - 1)
    row_blk = lambda w, idx, **kw: pl.BlockSpec((tf, w), lambda s: (idx(s), 0), **kw)
    halo = lambda idx, **kw: pl.BlockSpec((POOL_HALO, POOL_WIDTH), lambda s: (idx(s), 0), **kw)
    first = lambda s: 0
    single = dict(pipeline_mode=pl.Buffered(1))
    once = lambda shape: pl.BlockSpec(shape, lambda s: (0,) * len(shape), **single)
    u2 = u.reshape(rows, POOL_WIDTH)
    attn2 = attn.reshape(rows, ATTN_WIDTH)
    xr = x.reshape(rows, D_MODEL)
    x2 = pl.pallas_call(
        functools.partial(_mixffn_kernel, seq=seq, alpha=alpha),
        grid=(n_steps,),
        in_specs=[row_blk(POOL_WIDTH, ahead),
                  halo(lambda s: jnp.maximum(ahead(s) * halo_blocks - 1, 0)),
                  halo(lambda s: jnp.minimum((ahead(s) + 1) * halo_blocks, last_halo)),
                  row_blk(ATTN_WIDTH, ahead), row_blk(D_MODEL, ahead),
                  row_blk(POOL_WIDTH, first, **single), halo(lambda s: halo_blocks, **single),
                  row_blk(ATTN_WIDTH, first, **single), row_blk(D_MODEL, first, **single),
                  once((len(POOL_WINDOWS), POOL_GROUP_DIM, POOL_GROUP_DIM)), once((1, POOL_WIDTH)),
                  once((D_MODEL, D_MODEL)), once((1, D_MODEL)), once((1, D_MODEL)),
                  once((D_MODEL, FFN_HIDDEN)), once((D_MODEL, FFN_HIDDEN)), once((FFN_HIDDEN, D_MODEL)),
                  once((1, D_MODEL)), once((1, D_MODEL))],
        out_specs=pl.BlockSpec((tf, D_MODEL), lambda s: (s, 0)),
        out_shape=jax.ShapeDtypeStruct((rows, D_MODEL), F32),
        scratch_shapes=[pltpu.VMEM((tf + 2 * POOL_HALO, POOL_WIDTH), F32),
                        pltpu.VMEM((2, tf, D_MODEL), F32)],
        compiler_params=_params(("arbitrary",)),
        name="mixffn",
    )(u2, u2, u2, attn2, xr, u2, u2, attn2, xr,
      pool_w.astype(BF16), pool_scale[None, :], w_o.astype(BF16), ln1_g[None, :], ln1_b[None, :],
      wg16, wu16, wd16, ln2_g[None, :], ln2_b[None, :])
    return x2.reshape(bsz, seq, D_MODEL)


def kernel(x, positions, w_in, pool_w, pool_scale, q_norm_g, w_q_up, kv_norm_g, w_k_up, w_v_up, w_o, ln1_g, ln1_b, w_gate, w_up, w_down, ln2_g, ln2_b):
    depth = w_in.shape[0]
    alpha = (2.0 * depth) ** 0.25
    inv_freq = 1.0 / (ROPE_THETA ** (jnp.arange(0, QK_ROPE_DIM, 2, dtype=F32) / QK_ROPE_DIM))
    tokens = positions.size
    plane = (ROPE_HALF, tokens // LANES, LANES)
    cos_t, sin_t = pl.pallas_call(
        _rope_table_kernel,
        in_specs=[pl.BlockSpec(memory_space=pltpu.SMEM), _full(plane[1:])],
        out_specs=[_full(plane), _full(plane)],
        out_shape=[jax.ShapeDtypeStruct(plane, F32)] * 2,
        grid=(1,),
        name="rope_table",
    )(inv_freq, positions.reshape(plane[1:]))
    cos_t = cos_t.reshape(ROPE_HALF, tokens)
    sin_t = sin_t.reshape(ROPE_HALF, tokens)
    for l in range(depth):
        x = _layer(x, cos_t, sin_t, w_in[l], pool_w[l], pool_scale[l], q_norm_g[l], w_q_up[l],
                   kv_norm_g[l], w_k_up[l], w_v_up[l], w_o[l], ln1_g[l], ln1_b[l],
                   w_gate[l], w_up[l], w_down[l], ln2_g[l], ln2_b[l], alpha)
    return x
```

```python
import functools
import math

import jax
import jax.numpy as jnp
from jax import lax
from jax.experimental import pallas as pl
from jax.experimental.pallas import tpu as pltpu

D_MODEL = 1024
POOL_WIDTH = 512
POOL_WINDOWS = (2, 4, 8, 16)
POOL_GROUP_DIM = 128
N_HEADS = 8
QK_NOPE_DIM = 64
QK_ROPE_DIM = 32
V_HEAD_DIM = 64
QK_HEAD_DIM = QK_NOPE_DIM + QK_ROPE_DIM
ATTN_WIDTH = N_HEADS * V_HEAD_DIM
Q_LORA_RANK = 384
KV_LORA_RANK = 256
ROPE_THETA = 10000.0
FFN_HIDDEN = 2816
LN_EPS = 1e-5
RMS_EPS = 1e-6

LANES = 128
SUBLANES = 8
BF16_ROWS = 2 * SUBLANES
MXU_COLS = 256
V_SLOT = V_HEAD_DIM + BF16_ROWS
VT_ROWS = N_HEADS * V_SLOT
HEAD_SLOT = LANES
ROPE_LO = QK_NOPE_DIM
ROPE_HALF = QK_ROPE_DIM // 2
H_WIDTH = POOL_WIDTH + Q_LORA_RANK + KV_LORA_RANK + LANES
POOL_HALO = 8
VMEM_LIMIT = 56 * 1024 * 1024

PROJ_TILE = 512
ATTN_Q_TILE = 512
ATTN_KEY_CHUNK = 256
ATTN_HEADS_PER_STEP = 8
ATTN_SKEW = 3
FFN_TILE = 512
FFN_SUBTILE = 256
assert FFN_TILE % FFN_SUBTILE == 0
FFN_CHUNKS = ((0, 768), (768, 1536), (1536, 2304), (2304, FFN_HIDDEN))
assert all(lo % MXU_COLS == 0 for lo, _ in FFN_CHUNKS)

BF16 = jnp.bfloat16
F32 = jnp.float32


def _rope(t, cos, sin_signed):
    partner = pltpu.roll(t, LANES - ROPE_HALF, 1)
    return t * cos + partner * sin_signed


def _rms(t, g):
    y = t * lax.rsqrt(jnp.mean(t * t, axis=-1, keepdims=True) + RMS_EPS)
    return y * g


def _layer_norm(y, g, b):
    mu = jnp.mean(y, axis=-1, keepdims=True)
    d = y - mu
    var = jnp.mean(d * d, axis=-1, keepdims=True)
    return d * lax.rsqrt(var + LN_EPS) * g + b


def _rope_table_kernel(invf_ref, pos_ref, cos_ref, sin_ref):
    pos = pos_ref[...].astype(F32)
    for f in range(ROPE_HALF):
        ang = pos * invf_ref[f]
        cos_ref[f] = jnp.cos(ang)
        sin_ref[f] = jnp.sin(ang)


def _rope_lanes(t, outside):
    tm = t.shape[1]
    full = jnp.concatenate([jnp.full((ROPE_LO, tm), outside, F32), t, t,
                            jnp.zeros((LANES - ROPE_LO - QK_ROPE_DIM, tm), F32)], axis=0)
    return full.T


def _proj_kernel(x_ref, cos_ref, sin_ref, w_in_ref, qg_ref, wq_ref, kvg_ref, wk_ref, wv_ref, ones_ref,
                 u_ref, q_ref, k_ref, v_ref, *, q_scale):
    x = x_ref[0].astype(BF16)
    h = jnp.dot(x, w_in_ref[...], preferred_element_type=F32)
    u_ref[0] = h[:, :POOL_WIDTH]
    o_kv = POOL_WIDTH + Q_LORA_RANK
    o_kr = o_kv + KV_LORA_RANK
    cq = _rms(h[:, POOL_WIDTH:o_kv], qg_ref[...])
    ckv = _rms(h[:, o_kv:o_kr], kvg_ref[...])
    kr = h[:, o_kr:]

    cos_t = cos_ref[...]
    sin_t = sin_ref[...]
    cos = _rope_lanes(cos_t, 1.0)
    sin = _rope_lanes(sin_t, 0.0)
    first_half = lax.broadcasted_iota(jnp.int32, cos.shape, 1) < ROPE_LO + ROPE_HALF
    sin = jnp.where(first_half, -sin, sin)

    cq_b = cq.astype(BF16)
    ckv_b = ckv.astype(BF16)
    nt_dims = (((1,), (1,)), ((), ()))
    qt = lax.dot_general(wq_ref[...], cq_b, nt_dims, preferred_element_type=F32)
    k_nope = jnp.dot(ckv_b, wk_ref[...], preferred_element_type=F32)
    vt = lax.dot_general(wv_ref[...], ckv_b, nt_dims, preferred_element_type=F32)
    v_ref[0] = (vt + ones_ref[...]).astype(BF16)

    cos_q = cos_t * q_scale
    sin_q = sin_t * q_scale
    k_rope = _rope(kr, cos, sin)
    for hd in range(N_HEADS):
        sl = slice(hd * HEAD_SLOT, (hd + 1) * HEAD_SLOT)
        k_ref[0, :, sl] = (k_nope[:, sl] + k_rope).astype(BF16)
        base = hd * HEAD_SLOT
        r1 = qt[base + ROPE_LO:base + ROPE_LO + ROPE_HALF]
        r2 = qt[base + ROPE_LO + ROPE_HALF:base + ROPE_LO + QK_ROPE_DIM]
        q_ref[0, base:base + ROPE_LO, :] = (qt[base:base + ROPE_LO] * q_scale).astype(BF16)
        q_ref[0, base + ROPE_LO:base + HEAD_SLOT, :] = jnp.concatenate(
            [r1 * cos_q - r2 * sin_q, r2 * cos_q + r1 * sin_q,
             jnp.zeros((HEAD_SLOT - QK_HEAD_DIM, qt.shape[1]), F32)], axis=0).astype(BF16)


def _row_groups(t, op):
    return op(t.reshape(t.shape[0] // SUBLANES, SUBLANES, t.shape[1]), axis=0)


def _attn_kernel(q_ref, qn_ref, k_ref, vt_ref, zero_ref, w32_a, w32_b, w32_c,
                 o_ref, w16_a, w16_b, w16_c, s_ref, mx_ref, *, kc):
    seq = k_ref.shape[1]
    hp = k_ref.shape[2] // HEAD_SLOT
    n_chunks = seq // kc
    assert hp % 2 == 0

    def score_chunk(q, h, c):
        hs = slice(h * HEAD_SLOT, (h + 1) * HEAD_SLOT)
        ks = slice(c * kc, (c + 1) * kc)
        s = jnp.dot(k_ref[0, ks, hs], q[0, hs, :], preferred_element_type=F32)
        s_ref[h % 2, ks, :] = s
        return _row_groups(s, jnp.max)

    def fold(cms):
        mx = cms[0]
        for cm in cms[1:]:
            mx = jnp.maximum(mx, cm)
        return mx

    @pl.when(pl.program_id(2) == 0)
    def _():
        mx_ref[...] = fold([score_chunk(q_ref, 0, c) for c in range(n_chunks)])

    mx = mx_ref[...]
    zero_bits = zero_ref[0:1, :]
    outs = []
    for h in range(hp):
        m = jnp.max(mx, axis=0, keepdims=True)
        ds = slice(h * V_SLOT, (h + 1) * V_SLOT)
        cms = []
        acc = None
        for c in range(n_chunks):
            ks = slice(c * kc, (c + 1) * kc)
            cms.append(score_chunk(q_ref, h + 1, c) if h + 1 < hp else score_chunk(qn_ref, 0, c))
            m_c = m
            if c >= ATTN_SKEW:
                dep = lax.bitcast_convert_type(cms[c - ATTN_SKEW][0:1, :], jnp.int32) & zero_bits
                m_c = m + lax.bitcast_convert_type(dep, F32)
            p = jnp.exp2(s_ref[h % 2, ks, :] - m_c)
            o = jnp.dot(vt_ref[0, ds, ks], p.astype(BF16), preferred_element_type=F32)
            acc = o if acc is None else acc + o
        outs.append(acc[:V_HEAD_DIM] / acc[V_HEAD_DIM:V_HEAD_DIM + 1])
        mx = fold(cms)
    mx_ref[...] = mx
    o_ref[0] = jnp.concatenate(outs, axis=0).T.astype(BF16)
    for w32, w16 in ((w32_a, w16_a), (w32_b, w16_b), (w32_c, w16_c)):
        w16[...] = w32[...].astype(BF16)


def _fill_halo_buffer(i, n_tiles, u_ref, uprev_ref, unext_ref, ext_ref):
    tm = u_ref.shape[0]
    ext_ref[0:POOL_HALO] = jnp.where(i > 0, uprev_ref[...], 0.0)
    ext_ref[POOL_HALO:POOL_HALO + tm] = u_ref[...]
    ext_ref[POOL_HALO + tm:] = jnp.where(i < n_tiles - 1, unext_ref[...], 0.0)


def _mix_stages(emit, seq_row0, r0, th, seq, alpha, attn_ref, x_ref,
                pw_ref, ps_ref, wo_ref, g_ref, b_ref, ext_ref):
    row = seq_row0 + r0 + lax.broadcasted_iota(jnp.int32, (th, 1), 0)
    mixed = []
    for g, w in enumerate(POOL_WINDOWS):
        cols = slice(g * POOL_GROUP_DIM, (g + 1) * POOL_GROUP_DIM)
        back = w // 2
        win = ext_ref[pl.ds(POOL_HALO + r0 - back, th), cols]
        for d in range(1 - back, w - back):
            win = win + ext_ref[pl.ds(POOL_HALO + r0 + d, th), cols]
        cnt = (jnp.minimum(row + (w - back), seq) - jnp.maximum(row - back, 0)).astype(F32)
        pg = win / cnt - ext_ref[pl.ds(POOL_HALO + r0, th), cols]
        og = jnp.dot(pg.astype(BF16), pw_ref[g], preferred_element_type=F32) * ps_ref[:, cols]
        mixed.append(og.astype(BF16))
        yield
    mixed.append(attn_ref[r0:r0 + th, :])
    acc = jnp.dot(jnp.concatenate(mixed, axis=1), wo_ref[...], preferred_element_type=F32)
    y = alpha * x_ref[r0:r0 + th, :] + acc
    emit(_layer_norm(y, g_ref[...], b_ref[...]))
    yield


def _ffn_stages(emit, x, alpha, wg_ref, wu_ref, wd_ref, g_ref, b_ref):
    xb = x.astype(BF16)
    ffn = None
    for lo, hi in FFN_CHUNKS:
        gate = jnp.dot(xb, wg_ref[:, lo:hi], preferred_element_type=F32)
        up = jnp.dot(xb, wu_ref[:, lo:hi], preferred_element_type=F32)
        hid = gate * (1.0 / (1.0 + jnp.exp(-gate))) * up
        part = jnp.dot(hid.astype(BF16), wd_ref[lo:hi, :], preferred_element_type=F32)
        ffn = part if ffn is None else ffn + part
        yield
    y = alpha * x + ffn
    emit(_layer_norm(y, g_ref[...], b_ref[...]))
    yield


def _mixffn_kernel(u_ref, uprev_ref, unext_ref, attn_ref, x_ref,
                   u0_ref, unext0_ref, attn0_ref, x0_ref,
                   pw_ref, ps_ref, wo_ref, g1_ref, b1_ref, wg_ref, wu_ref, wd_ref, g2_ref, b2_ref,
                   out_ref, ext_ref, x1_ref, *, seq, alpha):
    s = pl.program_id(0)
    n_steps = pl.num_programs(0)
    tm = u_ref.shape[0]
    n_tiles = seq // tm
    th = FFN_SUBTILE
    mix_w = (pw_ref, ps_ref, wo_ref, g1_ref, b1_ref, ext_ref)

    def store(ref, slot, r0):
        def emit(v):
            ref[slot, r0:r0 + th, :] = v
        return emit

    def store_out(r0):
        def emit(v):
            out_ref[r0:r0 + th, :] = v
        return emit

    @pl.when(s == 0)
    def _():
        _fill_halo_buffer(0, n_tiles, u0_ref, unext0_ref, unext0_ref, ext_ref)
        for r0 in range(0, tm, th):
            for _ in _mix_stages(store(x1_ref, 0, r0), 0, r0, th, seq, alpha, attn0_ref, x0_ref, *mix_w):
                pass

    nxt = jnp.minimum(s + 1, n_steps - 1)
    x_cur = [x1_ref[s % 2, r0:r0 + th, :] for r0 in range(0, tm, th)]
    _fill_halo_buffer(nxt % n_tiles, n_tiles, u_ref, uprev_ref, unext_ref, ext_ref)
    for j, r0 in enumerate(range(0, tm, th)):
        ffn = _ffn_stages(store_out(r0), x_cur[j], alpha, wg_ref, wu_ref, wd_ref, g2_ref, b2_ref)
        mix = _mix_stages(store(x1_ref, (s + 1) % 2, r0), (nxt % n_tiles) * tm, r0, th, seq, alpha,
                          attn_ref, x_ref, *mix_w)
        for _ in range(len(FFN_CHUNKS) + 1):
            next(ffn)
            next(mix)
        assert next(mix, None) is None and next(ffn, None) is None


def _full(shape):
    return pl.BlockSpec(shape, lambda *_: (0,) * len(shape))


def _params(sem):
    return pltpu.CompilerParams(dimension_semantics=sem, vmem_limit_bytes=VMEM_LIMIT)


def _pad_heads(w, dim):
    r = w.shape[0]
    w = w.reshape(r, N_HEADS, dim)
    return jnp.pad(w, ((0, 0), (0, 0), (0, HEAD_SLOT - dim))).reshape(r, N_HEADS * HEAD_SLOT)


def _rope_slot(w):
    return jnp.concatenate([w, w[..., :ROPE_HALF], jnp.zeros_like(w[..., :ROPE_HALF])], axis=-1)


def _layer(x, cos_t, sin_t, w_in, pool_w, pool_scale, q_norm_g, w_q_up, kv_norm_g, w_k_up, w_v_up,
           w_o, ln1_g, ln1_b, w_gate, w_up, w_down, ln2_g, ln2_b, alpha):
    bsz, seq, _ = x.shape
    tm, tq, tf, hp = PROJ_TILE, ATTN_Q_TILE, FFN_TILE, ATTN_HEADS_PER_STEP
    assert seq % tm == 0 and seq % tq == 0 and seq % tf == 0 and N_HEADS % hp == 0
    assert len(POOL_WINDOWS) == len(FFN_CHUNKS)
    o_kr = POOL_WIDTH + Q_LORA_RANK + KV_LORA_RANK

    w_in_ext = jnp.concatenate(
        [w_in[:, :o_kr], jnp.zeros((D_MODEL, ROPE_LO), w_in.dtype), _rope_slot(w_in[:, o_kr:])],
        axis=1).astype(BF16)
    wq = _pad_heads(w_q_up, QK_HEAD_DIM).T.astype(BF16)
    wk = _pad_heads(w_k_up, QK_NOPE_DIM).astype(BF16)
    wv = jnp.pad(w_v_up.T.reshape(N_HEADS, V_HEAD_DIM, KV_LORA_RANK),
                 ((0, 0), (0, V_SLOT - V_HEAD_DIM), (0, 0))).reshape(VT_ROWS, KV_LORA_RANK).astype(BF16)
    ones_col = (jnp.arange(VT_ROWS) % V_SLOT == V_HEAD_DIM).astype(F32)[:, None]
    q_scale = (QK_HEAD_DIM ** -0.5) * math.log2(math.e)

    tok = lambda w: pl.BlockSpec((1, tm, w), lambda b, i: (b, i, 0))
    rope = lambda: pl.BlockSpec((ROPE_HALF, tm), lambda b, i: (0, b * (seq // tm) + i))
    u, q, k, v = pl.pallas_call(
        functools.partial(_proj_kernel, q_scale=q_scale),
        grid=(bsz, seq // tm),
        in_specs=[tok(D_MODEL), rope(), rope(), _full((D_MODEL, H_WIDTH)),
                  _full((1, Q_LORA_RANK)), _full((N_HEADS * HEAD_SLOT, Q_LORA_RANK)),
                  _full((1, KV_LORA_RANK)), _full((KV_LORA_RANK, N_HEADS * HEAD_SLOT)),
                  _full((VT_ROWS, KV_LORA_RANK)), _full((VT_ROWS, 1))],
        out_specs=[tok(POOL_WIDTH),
                   pl.BlockSpec((1, N_HEADS * HEAD_SLOT, tm), lambda b, i: (b, 0, i)),
                   tok(N_HEADS * HEAD_SLOT),
                   pl.BlockSpec((1, VT_ROWS, tm), lambda b, i: (b, 0, i))],
        out_shape=[jax.ShapeDtypeStruct((bsz, seq, POOL_WIDTH), F32),
                   jax.ShapeDtypeStruct((bsz, N_HEADS * HEAD_SLOT, seq), BF16),
                   jax.ShapeDtypeStruct((bsz, seq, N_HEADS * HEAD_SLOT), BF16),
                   jax.ShapeDtypeStruct((bsz, VT_ROWS, seq), BF16)],
        compiler_params=_params(("parallel", "parallel")),
        name="proj",
    )(x, cos_t, sin_t, w_in_ext, q_norm_g[None, :], wq, kv_norm_g[None, :], wk, wv, ones_col)

    n_groups, n_q = N_HEADS // hp, seq // tq
    steps = bsz * n_groups * n_q
    w_rows, wd_rows = D_MODEL // steps, FFN_HIDDEN // (steps // 2)
    assert w_rows * steps == D_MODEL and wd_rows * (steps // 2) == FFN_HIDDEN
    assert w_rows % BF16_ROWS == 0 and wd_rows % BF16_ROWS == 0
    step_of = lambda b, g, i: (b * n_groups + g) * n_q + i
    w_slice = pl.BlockSpec((w_rows, FFN_HIDDEN), lambda b, g, i: (step_of(b, g, i), 0))
    wd_slice = pl.BlockSpec((wd_rows, D_MODEL), lambda b, g, i: (step_of(b, g, i) // 2, 0))
    attn, wg16, wu16, wd16 = pl.pallas_call(
        functools.partial(_attn_kernel, kc=ATTN_KEY_CHUNK),
        grid=(bsz, n_groups, n_q),
        in_specs=[pl.BlockSpec((1, hp * HEAD_SLOT, tq), lambda b, g, i: (b, g, i)),
                  pl.BlockSpec((1, hp * HEAD_SLOT, tq),
                               lambda b, g, i: (b, g, jnp.minimum(i + 1, n_q - 1))),
                  pl.BlockSpec((1, seq, hp * HEAD_SLOT), lambda b, g, i: (b, 0, g)),
                  pl.BlockSpec((1, hp * V_SLOT, seq), lambda b, g, i: (b, g, 0)),
                  _full((SUBLANES, tq)), w_slice, w_slice, wd_slice],
        out_specs=[pl.BlockSpec((1, tq, hp * V_HEAD_DIM), lambda b, g, i: (b, i, g)),
                   w_slice, w_slice, wd_slice],
        out_shape=[jax.ShapeDtypeStruct((bsz, seq, ATTN_WIDTH), BF16)]
                  + [jax.ShapeDtypeStruct(w.shape, BF16) for w in (w_gate, w_up, w_down)],
        scratch_shapes=[pltpu.VMEM((2, seq, tq), F32),
                        pltpu.VMEM((SUBLANES, tq), F32)],
        compiler_params=_params(("arbitrary", "arbitrary", "arbitrary")),
        name="attn",
    )(q, q, k, v, jnp.zeros((SUBLANES, tq), jnp.int32), w_gate, w_up, w_down)

    rows = bsz * seq
    n_steps = rows // tf
    halo_blocks = tf // POOL_HALO
    last_halo = rows // POOL_HALO - 1
    ahead = lambda s: jnp.minimum(s + 1, n_steps - 1)
    row_blk = lambda w, idx, **kw: pl.BlockSpec((tf, w), lambda s: (idx(s), 0), **kw)
    halo = lambda idx, **kw: pl.BlockSpec((POOL_HALO, POOL_WIDTH), lambda s: (idx(s), 0), **kw)
    first = lambda s: 0
    single = dict(pipeline_mode=pl.Buffered(1))
    once = lambda shape: pl.BlockSpec(shape, lambda s: (0,) * len(shape), **single)
    u2 = u.reshape(rows, POOL_WIDTH)
    attn2 = attn.reshape(rows, ATTN_WIDTH)
    xr = x.reshape(rows, D_MODEL)
    x2 = pl.pallas_call(
        functools.partial(_mixffn_kernel, seq=seq, alpha=alpha),
        grid=(n_steps,),
        in_specs=[row_blk(POOL_WIDTH, ahead),
                  halo(lambda s: jnp.maximum(ahead(s) * halo_blocks - 1, 0)),
                  halo(lambda s: jnp.minimum((ahead(s) + 1) * halo_blocks, last_halo)),
                  row_blk(ATTN_WIDTH, ahead), row_blk(D_MODEL, ahead),
                  row_blk(POOL_WIDTH, first, **single), halo(lambda s: halo_blocks, **single),
                  row_blk(ATTN_WIDTH, first, **single), row_blk(D_MODEL, first, **single),
                  once((len(POOL_WINDOWS), POOL_GROUP_DIM, POOL_GROUP_DIM)), once((1, POOL_WIDTH)),
                  once((D_MODEL, D_MODEL)), once((1, D_MODEL)), once((1, D_MODEL)),
                  once((D_MODEL, FFN_HIDDEN)), once((D_MODEL, FFN_HIDDEN)), once((FFN_HIDDEN, D_MODEL)),
                  once((1, D_MODEL)), once((1, D_MODEL))],
        out_specs=pl.BlockSpec((tf, D_MODEL), lambda s: (s, 0)),
        out_shape=jax.ShapeDtypeStruct((rows, D_MODEL), F32),
        scratch_shapes=[pltpu.VMEM((tf + 2 * POOL_HALO, POOL_WIDTH), F32),
                        pltpu.VMEM((2, tf, D_MODEL), F32)],
        compiler_params=_params(("arbitrary",)),
        name="mixffn",
    )(u2, u2, u2, attn2, xr, u2, u2, attn2, xr,
      pool_w.astype(BF16), pool_scale[None, :], w_o.astype(BF16), ln1_g[None, :], ln1_b[None, :],
      wg16, wu16, wd16, ln2_g[None, :], ln2_b[None, :])
    return x2.reshape(bsz, seq, D_MODEL)


def kernel(x, positions, w_in, pool_w, pool_scale, q_norm_g, w_q_up, kv_norm_g, w_k_up, w_v_up, w_o, ln1_g, ln1_b, w_gate, w_up, w_down, ln2_g, ln2_b):
    depth = w_in.shape[0]
    alpha = (2.0 * depth) ** 0.25
    inv_freq = 1.0 / (ROPE_THETA ** (jnp.arange(0, QK_ROPE_DIM, 2, dtype=F32) / QK_ROPE_DIM))
    tokens = positions.size
    plane = (ROPE_HALF, tokens // LANES, LANES)
    cos_t, sin_t = pl.pallas_call(
        _rope_table_kernel,
        in_specs=[pl.BlockSpec(memory_space=pltpu.SMEM), _full(plane[1:])],
        out_specs=[_full(plane), _full(plane)],
        out_shape=[jax.ShapeDtypeStruct(plane, F32)] * 2,
        grid=(1,),
        name="rope_table",
    )(inv_freq, positions.reshape(plane[1:]))
    cos_t = cos_t.reshape(ROPE_HALF, tokens)
    sin_t = sin_t.reshape(ROPE_HALF, tokens)
    for l in range(depth):
        x = _layer(x, cos_t, sin_t, w_in[l], pool_w[l], pool_scale[l], q_norm_g[l], w_q_up[l],
                   kv_norm_g[l], w_k_up[l], w_v_up[l], w_o[l], ln1_g[l], ln1_b[l],
                   w_gate[l], w_up[l], w_down[l], ln2_g[l], ln2_b[l], alpha)
    return x
```

```python
import functools
import math

import jax
import jax.numpy as jnp
from jax import lax
from jax.experimental import pallas as pl
from jax.experimental.pallas import tpu as pltpu

D_MODEL = 1024
POOL_WIDTH = 512
POOL_WINDOWS = (2, 4, 8, 16)
POOL_GROUP_DIM = 128
N_HEADS = 8
QK_NOPE_DIM = 64
QK_ROPE_DIM = 32
V_HEAD_DIM = 64
QK_HEAD_DIM = QK_NOPE_DIM + QK_ROPE_DIM
ATTN_WIDTH = N_HEADS * V_HEAD_DIM
Q_LORA_RANK = 384
KV_LORA_RANK = 256
ROPE_THETA = 10000.0
FFN_HIDDEN = 2816
LN_EPS = 1e-5
RMS_EPS = 1e-6

LANES = 128
SUBLANES = 8
BF16_ROWS = 2 * SUBLANES
MXU_COLS = 256
V_SLOT = V_HEAD_DIM + BF16_ROWS
VT_ROWS = N_HEADS * V_SLOT
HEAD_SLOT = LANES
ROPE_LO = QK_NOPE_DIM
ROPE_HALF = QK_ROPE_DIM // 2
H_WIDTH = POOL_WIDTH + Q_LORA_RANK + KV_LORA_RANK + LANES
POOL_HALO = 8
VMEM_LIMIT = 56 * 1024 * 1024

PROJ_TILE = 512
ATTN_Q_TILE = 512
ATTN_KEY_CHUNK = 256
ATTN_HEADS_PER_STEP = 8
ATTN_SKEW = 3
FFN_TILE = 512
FFN_SUBTILE = 256
assert FFN_TILE % FFN_SUBTILE == 0
FFN_CHUNKS = ((0, 768), (768, 1536), (1536, 2304), (2304, FFN_HIDDEN))
assert all(lo % MXU_COLS == 0 for lo, _ in FFN_CHUNKS)

BF16 = jnp.bfloat16
F32 = jnp.float32


def _rope(t, cos, sin_signed):
    partner = pltpu.roll(t, LANES - ROPE_HALF, 1)
    return t * cos + partner * sin_signed


def _rms(t, g):
    y = t * lax.rsqrt(jnp.mean(t * t, axis=-1, keepdims=True) + RMS_EPS)
    return y * g


def _layer_norm(y, g, b):
    mu = jnp.mean(y, axis=-1, keepdims=True)
    d = y - mu
    var = jnp.mean(d * d, axis=-1, keepdims=True)
    return d * lax.rsqrt(var + LN_EPS) * g + b


def _rope_table_kernel(invf_ref, pos_ref, cos_ref, sin_ref):
    pos = pos_ref[...].astype(F32)
    for f in range(ROPE_HALF):
        ang = pos * invf_ref[f]
        cos_ref[:, f, :] = jnp.cos(ang)
        sin_ref[:, f, :] = jnp.sin(ang)


def _rope_lanes(t, outside):
    tm = t.shape[1]
    full = jnp.concatenate([jnp.full((ROPE_LO, tm), outside, F32), t, t,
                            jnp.zeros((LANES - ROPE_LO - QK_ROPE_DIM, tm), F32)], axis=0)
    return full.T


def _proj_kernel(x_ref, cos_ref, sin_ref, w_in_ref, qg_ref, wq_ref, kvg_ref, wk_ref, wv_ref, ones_ref,
                 u_ref, q_ref, k_ref, v_ref, *, q_scale):
    x = x_ref[0].astype(BF16)
    h = jnp.dot(x, w_in_ref[...], preferred_element_type=F32)
    u_ref[0] = h[:, :POOL_WIDTH]
    o_kv = POOL_WIDTH + Q_LORA_RANK
    o_kr = o_kv + KV_LORA_RANK
    cq = _rms(h[:, POOL_WIDTH:o_kv], qg_ref[...])
    ckv = _rms(h[:, o_kv:o_kr], kvg_ref[...])
    kr = h[:, o_kr:]

    groups = cos_ref.shape[0]
    cos_t = jnp.concatenate([cos_ref[j] for j in range(groups)], axis=1)
    sin_t = jnp.concatenate([sin_ref[j] for j in range(groups)], axis=1)
    cos = _rope_lanes(cos_t, 1.0)
    sin = _rope_lanes(sin_t, 0.0)
    first_half = lax.broadcasted_iota(jnp.int32, cos.shape, 1) < ROPE_LO + ROPE_HALF
    sin = jnp.where(first_half, -sin, sin)

    cq_b = cq.astype(BF16)
    ckv_b = ckv.astype(BF16)
    nt_dims = (((1,), (1,)), ((), ()))
    qt = lax.dot_general(wq_ref[...], cq_b, nt_dims, preferred_element_type=F32)
    k_nope = jnp.dot(ckv_b, wk_ref[...], preferred_element_type=F32)
    vt = lax.dot_general(wv_ref[...], ckv_b, nt_dims, preferred_element_type=F32)
    v_ref[0] = (vt + ones_ref[...]).astype(BF16)

    cos_q = cos_t * q_scale
    sin_q = sin_t * q_scale
    k_rope = _rope(kr, cos, sin)
    for hd in range(N_HEADS):
        sl = slice(hd * HEAD_SLOT, (hd + 1) * HEAD_SLOT)
        k_ref[0, :, sl] = (k_nope[:, sl] + k_rope).astype(BF16)
        base = hd * HEAD_SLOT
        r1 = qt[base + ROPE_LO:base + ROPE_LO + ROPE_HALF]
        r2 = qt[base + ROPE_LO + ROPE_HALF:base + ROPE_LO + QK_ROPE_DIM]
        q_ref[0, base:base + ROPE_LO, :] = (qt[base:base + ROPE_LO] * q_scale).astype(BF16)
        q_ref[0, base + ROPE_LO:base + HEAD_SLOT, :] = jnp.concatenate(
            [r1 * cos_q - r2 * sin_q, r2 * cos_q + r1 * sin_q,
             jnp.zeros((HEAD_SLOT - QK_HEAD_DIM, qt.shape[1]), F32)], axis=0).astype(BF16)


def _row_groups(t, op):
    return op(t.reshape(t.shape[0] // SUBLANES, SUBLANES, t.shape[1]), axis=0)


def _attn_kernel(q_ref, qn_ref, k_ref, vt_ref, zero_ref, w32_a, w32_b, w32_c,
                 o_ref, w16_a, w16_b, w16_c, s_ref, mx_ref, *, kc):
    seq = k_ref.shape[1]
    hp = k_ref.shape[2] // HEAD_SLOT
    n_chunks = seq // kc
    assert hp % 2 == 0

    def score_chunk(q, h, c):
        hs = slice(h * HEAD_SLOT, (h + 1) * HEAD_SLOT)
        ks = slice(c * kc, (c + 1) * kc)
        s = jnp.dot(k_ref[0, ks, hs], q[0, hs, :], preferred_element_type=F32)
        s_ref[h % 2, ks, :] = s
        return _row_groups(s, jnp.max)

    def fold(cms):
        mx = cms[0]
        for cm in cms[1:]:
            mx = jnp.maximum(mx, cm)
        return mx

    @pl.when(pl.program_id(2) == 0)
    def _():
        mx_ref[...] = fold([score_chunk(q_ref, 0, c) for c in range(n_chunks)])

    mx = mx_ref[...]
    zero_bits = zero_ref[0:1, :]
    outs = []
    for h in range(hp):
        m = jnp.max(mx, axis=0, keepdims=True)
        ds = slice(h * V_SLOT, (h + 1) * V_SLOT)
        cms = []
        acc = None
        for c in range(n_chunks):
            ks = slice(c * kc, (c + 1) * kc)
            cms.append(score_chunk(q_ref, h + 1, c) if h + 1 < hp else score_chunk(qn_ref, 0, c))
            m_c = m
            if c >= ATTN_SKEW:
                dep = lax.bitcast_convert_type(cms[c - ATTN_SKEW][0:1, :], jnp.int32) & zero_bits
                m_c = m + lax.bitcast_convert_type(dep, F32)
            p = jnp.exp2(s_ref[h % 2, ks, :] - m_c)
            o = jnp.dot(vt_ref[0, ds, ks], p.astype(BF16), preferred_element_type=F32)
            acc = o if acc is None else acc + o
        outs.append(acc[:V_HEAD_DIM] / acc[V_HEAD_DIM:V_HEAD_DIM + 1])
        mx = fold(cms)
    mx_ref[...] = mx
    o_ref[0] = jnp.concatenate(outs, axis=0).T.astype(BF16)
    for w32, w16 in ((w32_a, w16_a), (w32_b, w16_b), (w32_c, w16_c)):
        w16[...] = w32[...].astype(BF16)


def _fill_halo_buffer(i, n_tiles, u_ref, uprev_ref, unext_ref, ext_ref):
    tm = u_ref.shape[0]
    ext_ref[0:POOL_HALO] = jnp.where(i > 0, uprev_ref[...], 0.0)
    ext_ref[POOL_HALO:POOL_HALO + tm] = u_ref[...]
    ext_ref[POOL_HALO + tm:] = jnp.where(i < n_tiles - 1, unext_ref[...], 0.0)


def _mix_stages(emit, seq_row0, r0, th, seq, alpha, attn_ref, x_ref,
                pw_ref, ps_ref, wo_ref, g_ref, b_ref, ext_ref):
    row = seq_row0 + r0 + lax.broadcasted_iota(jnp.int32, (th, 1), 0)
    mixed = []
    for g, w in enumerate(POOL_WINDOWS):
        cols = slice(g * POOL_GROUP_DIM, (g + 1) * POOL_GROUP_DIM)
        back = w // 2
        win = ext_ref[pl.ds(POOL_HALO + r0 - back, th), cols]
        for d in range(1 - back, w - back):
            win = win + ext_ref[pl.ds(POOL_HALO + r0 + d, th), cols]
        cnt = (jnp.minimum(row + (w - back), seq) - jnp.maximum(row - back, 0)).astype(F32)
        pg = win / cnt - ext_ref[pl.ds(POOL_HALO + r0, th), cols]
        og = jnp.dot(pg.astype(BF16), pw_ref[g], preferred_element_type=F32) * ps_ref[:, cols]
        mixed.append(og.astype(BF16))
        yield
    mixed.append(attn_ref[r0:r0 + th, :])
    acc = jnp.dot(jnp.concatenate(mixed, axis=1), wo_ref[...], preferred_element_type=F32)
    y = alpha * x_ref[r0:r0 + th, :] + acc
    emit(_layer_norm(y, g_ref[...], b_ref[...]))
    yield


def _ffn_stages(emit, x, alpha, wg_ref, wu_ref, wd_ref, g_ref, b_ref):
    xb = x.astype(BF16)
    ffn = None
    for lo, hi in FFN_CHUNKS:
        gate = jnp.dot(xb, wg_ref[:, lo:hi], preferred_element_type=F32)
        up = jnp.dot(xb, wu_ref[:, lo:hi], preferred_element_type=F32)
        hid = gate * (1.0 / (1.0 + jnp.exp(-gate))) * up
        part = jnp.dot(hid.astype(BF16), wd_ref[lo:hi, :], preferred_element_type=F32)
        ffn = part if ffn is None else ffn + part
        yield
    y = alpha * x + ffn
    emit(_layer_norm(y, g_ref[...], b_ref[...]))
    yield


def _mixffn_kernel(u_ref, uprev_ref, unext_ref, attn_ref, x_ref,
                   u0_ref, unext0_ref, attn0_ref, x0_ref,
                   pw_ref, ps_ref, wo_ref, g1_ref, b1_ref, wg_ref, wu_ref, wd_ref, g2_ref, b2_ref,
                   out_ref, ext_ref, x1_ref, *, seq, alpha):
    s = pl.program_id(0)
    n_steps = pl.num_programs(0)
    tm = u_ref.shape[0]
    n_tiles = seq // tm
    th = FFN_SUBTILE
    mix_w = (pw_ref, ps_ref, wo_ref, g1_ref, b1_ref, ext_ref)

    def store(ref, slot, r0):
        def emit(v):
            ref[slot, r0:r0 + th, :] = v
        return emit

    def store_out(r0):
        def emit(v):
            out_ref[r0:r0 + th, :] = v
        return emit

    @pl.when(s == 0)
    def _():
        _fill_halo_buffer(0, n_tiles, u0_ref, unext0_ref, unext0_ref, ext_ref)
        for r0 in range(0, tm, th):
            for _ in _mix_stages(store(x1_ref, 0, r0), 0, r0, th, seq, alpha, attn0_ref, x0_ref, *mix_w):
                pass

    nxt = jnp.minimum(s + 1, n_steps - 1)
    x_cur = [x1_ref[s % 2, r0:r0 + th, :] for r0 in range(0, tm, th)]
    _fill_halo_buffer(nxt % n_tiles, n_tiles, u_ref, uprev_ref, unext_ref, ext_ref)
    for j, r0 in enumerate(range(0, tm, th)):
        ffn = _ffn_stages(store_out(r0), x_cur[j], alpha, wg_ref, wu_ref, wd_ref, g2_ref, b2_ref)
        mix = _mix_stages(store(x1_ref, (s + 1) % 2, r0), (nxt % n_tiles) * tm, r0, th, seq, alpha,
                          attn_ref, x_ref, *mix_w)
        for _ in range(len(FFN_CHUNKS) + 1):
            next(ffn)
            next(mix)
        assert next(mix, None) is None and next(ffn, None) is None


def _full(shape):
    return pl.BlockSpec(shape, lambda *_: (0,) * len(shape))


def _params(sem):
    return pltpu.CompilerParams(dimension_semantics=sem, vmem_limit_bytes=VMEM_LIMIT)


def _pad_heads(w, dim):
    r = w.shape[0]
    w = w.reshape(r, N_HEADS, dim)
    return jnp.pad(w, ((0, 0), (0, 0), (0, HEAD_SLOT - dim))).reshape(r, N_HEADS * HEAD_SLOT)


def _rope_slot(w):
    return jnp.concatenate([w, w[..., :ROPE_HALF], jnp.zeros_like(w[..., :ROPE_HALF])], axis=-1)


def _layer(x, cos_t, sin_t, w_in, pool_w, pool_scale, q_norm_g, w_q_up, kv_norm_g, w_k_up, w_v_up,
           w_o, ln1_g, ln1_b, w_gate, w_up, w_down, ln2_g, ln2_b, alpha):
    bsz, seq, _ = x.shape
    tm, tq, tf, hp = PROJ_TILE, ATTN_Q_TILE, FFN_TILE, ATTN_HEADS_PER_STEP
    assert seq % tm == 0 and seq % tq == 0 and seq % tf == 0 and N_HEADS % hp == 0
    assert len(POOL_WINDOWS) == len(FFN_CHUNKS)
    o_kr = POOL_WIDTH + Q_LORA_RANK + KV_LORA_RANK

    w_in_ext = jnp.concatenate(
        [w_in[:, :o_kr], jnp.zeros((D_MODEL, ROPE_LO), w_in.dtype), _rope_slot(w_in[:, o_kr:])],
        axis=1).astype(BF16)
    wq = _pad_heads(w_q_up, QK_HEAD_DIM).T.astype(BF16)
    wk = _pad_heads(w_k_up, QK_NOPE_DIM).astype(BF16)
    wv = jnp.pad(w_v_up.T.reshape(N_HEADS, V_HEAD_DIM, KV_LORA_RANK),
                 ((0, 0), (0, V_SLOT - V_HEAD_DIM), (0, 0))).reshape(VT_ROWS, KV_LORA_RANK).astype(BF16)
    ones_col = (jnp.arange(VT_ROWS) % V_SLOT == V_HEAD_DIM).astype(F32)[:, None]
    q_scale = (QK_HEAD_DIM ** -0.5) * math.log2(math.e)

    tok = lambda w: pl.BlockSpec((1, tm, w), lambda b, i: (b, i, 0))
    rope = lambda: pl.BlockSpec((tm // LANES, ROPE_HALF, LANES),
                                lambda b, i: (b * (seq // tm) + i, 0, 0))
    u, q, k, v = pl.pallas_call(
        functools.partial(_proj_kernel, q_scale=q_scale),
        grid=(bsz, seq // tm),
        in_specs=[tok(D_MODEL), rope(), rope(), _full((D_MODEL, H_WIDTH)),
                  _full((1, Q_LORA_RANK)), _full((N_HEADS * HEAD_SLOT, Q_LORA_RANK)),
                  _full((1, KV_LORA_RANK)), _full((KV_LORA_RANK, N_HEADS * HEAD_SLOT)),
                  _full((VT_ROWS, KV_LORA_RANK)), _full((VT_ROWS, 1))],
        out_specs=[tok(POOL_WIDTH),
                   pl.BlockSpec((1, N_HEADS * HEAD_SLOT, tm), lambda b, i: (b, 0, i)),
                   tok(N_HEADS * HEAD_SLOT),
                   pl.BlockSpec((1, VT_ROWS, tm), lambda b, i: (b, 0, i))],
        out_shape=[jax.ShapeDtypeStruct((bsz, seq, POOL_WIDTH), F32),
                   jax.ShapeDtypeStruct((bsz, N_HEADS * HEAD_SLOT, seq), BF16),
                   jax.ShapeDtypeStruct((bsz, seq, N_HEADS * HEAD_SLOT), BF16),
                   jax.ShapeDtypeStruct((bsz, VT_ROWS, seq), BF16)],
        compiler_params=_params(("parallel", "parallel")),
        name="proj",
    )(x, cos_t, sin_t, w_in_ext, q_norm_g[None, :], wq, kv_norm_g[None, :], wk, wv, ones_col)

    n_groups, n_q = N_HEADS // hp, seq // tq
    steps = bsz * n_groups * n_q
    w_rows, wd_rows = D_MODEL // steps, FFN_HIDDEN // (steps // 2)
    assert w_rows * steps == D_MODEL and wd_rows * (steps // 2) == FFN_HIDDEN
    assert w_rows % BF16_ROWS == 0 and wd_rows % BF16_ROWS == 0
    step_of = lambda b, g, i: (b * n_groups + g) * n_q + i
    w_slice = pl.BlockSpec((w_rows, FFN_HIDDEN), lambda b, g, i: (step_of(b, g, i), 0))
    wd_slice = pl.BlockSpec((wd_rows, D_MODEL), lambda b, g, i: (step_of(b, g, i) // 2, 0))
    attn, wg16, wu16, wd16 = pl.pallas_call(
        functools.partial(_attn_kernel, kc=ATTN_KEY_CHUNK),
        grid=(bsz, n_groups, n_q),
        in_specs=[pl.BlockSpec((1, hp * HEAD_SLOT, tq), lambda b, g, i: (b, g, i)),
                  pl.BlockSpec((1, hp * HEAD_SLOT, tq),
                               lambda b, g, i: (b, g, jnp.minimum(i + 1, n_q - 1))),
                  pl.BlockSpec((1, seq, hp * HEAD_SLOT), lambda b, g, i: (b, 0, g)),
                  pl.BlockSpec((1, hp * V_SLOT, seq), lambda b, g, i: (b, g, 0)),
                  _full((SUBLANES, tq)), w_slice, w_slice, wd_slice],
        out_specs=[pl.BlockSpec((1, tq, hp * V_HEAD_DIM), lambda b, g, i: (b, i, g)),
                   w_slice, w_slice, wd_slice],
        out_shape=[jax.ShapeDtypeStruct((bsz, seq, ATTN_WIDTH), BF16)]
                  + [jax.ShapeDtypeStruct(w.shape, BF16) for w in (w_gate, w_up, w_down)],
        scratch_shapes=[pltpu.VMEM((2, seq, tq), F32),
                        pltpu.VMEM((SUBLANES, tq), F32)],
        compiler_params=_params(("arbitrary", "arbitrary", "arbitrary")),
        name="attn",
    )(q, q, k, v, jnp.zeros((SUBLANES, tq), jnp.int32), w_gate, w_up, w_down)

    rows = bsz * seq
    n_steps = rows // tf
    halo_blocks = tf // POOL_HALO
    last_halo = rows // POOL_HALO - 1
    ahead = lambda s: jnp.minimum(s + 1, n_steps - 1)
    row_blk = lambda w, idx, **kw: pl.BlockSpec((tf, w), lambda s: (idx(s), 0), **kw)
    halo = lambda idx, **kw: pl.BlockSpec((POOL_HALO, POOL_WIDTH), lambda s: (idx(s), 0), **kw)
    first = lambda s: 0
    single = dict(pipeline_mode=pl.Buffered(1))
    once = lambda shape: pl.BlockSpec(shape, lambda s: (0,) * len(shape), **single)
    u2 = u.reshape(rows, POOL_WIDTH)
    attn2 = attn.reshape(rows, ATTN_WIDTH)
    xr = x.reshape(rows, D_MODEL)
    x2 = pl.pallas_call(
        functools.partial(_mixffn_kernel, seq=seq, alpha=alpha),
        grid=(n_steps,),
        in_specs=[row_blk(POOL_WIDTH, ahead),
                  halo(lambda s: jnp.maximum(ahead(s) * halo_blocks - 1, 0)),
                  halo(lambda s: jnp.minimum((ahead(s) + 1) * halo_blocks, last_halo)),
                  row_blk(ATTN_WIDTH, ahead), row_blk(D_MODEL, ahead),
                  row_blk(POOL_WIDTH, first, **single), halo(lambda s: halo_blocks, **single),
                  row_blk(ATTN_WIDTH, first, **single), row_blk(D_MODEL, first, **single),
                  once((len(POOL_WINDOWS), POOL_GROUP_DIM, POOL_GROUP_DIM)), once((1, POOL_WIDTH)),
                  once((D_MODEL, D_MODEL)), once((1, D_MODEL)), once((1, D_MODEL)),
                  once((D_MODEL, FFN_HIDDEN)), once((D_MODEL, FFN_HIDDEN)), once((FFN_HIDDEN, D_MODEL)),
                  once((1, D_MODEL)), once((1, D_MODEL))],
        out_specs=pl.BlockSpec((tf, D_MODEL), lambda s: (s, 0)),
        out_shape=jax.ShapeDtypeStruct((rows, D_MODEL), F32),
        scratch_shapes=[pltpu.VMEM((tf + 2 * POOL_HALO, POOL_WIDTH), F32),
                        pltpu.VMEM((2, tf, D_MODEL), F32)],
        compiler_params=_params(("arbitrary",)),
        name="mixffn",
    )(u2, u2, u2, attn2, xr, u2, u2, attn2, xr,
      pool_w.astype(BF16), pool_scale[None, :], w_o.astype(BF16), ln1_g[None, :], ln1_b[None, :],
      wg16, wu16, wd16, ln2_g[None, :], ln2_b[None, :])
    return x2.reshape(bsz, seq, D_MODEL)


def kernel(x, positions, w_in, pool_w, pool_scale, q_norm_g, w_q_up, kv_norm_g, w_k_up, w_v_up, w_o, ln1_g, ln1_b, w_gate, w_up, w_down, ln2_g, ln2_b):
    depth = w_in.shape[0]
    alpha = (2.0 * depth) ** 0.25
    inv_freq = 1.0 / (ROPE_THETA ** (jnp.arange(0, QK_ROPE_DIM, 2, dtype=F32) / QK_ROPE_DIM))
    tokens = positions.size
    groups = tokens // LANES
    table = (groups, ROPE_HALF, LANES)
    cos_t, sin_t = pl.pallas_call(
        _rope_table_kernel,
        in_specs=[pl.BlockSpec(memory_space=pltpu.SMEM), _full((groups, LANES))],
        out_specs=[_full(table), _full(table)],
        out_shape=[jax.ShapeDtypeStruct(table, F32)] * 2,
        grid=(1,),
        name="rope_table",
    )(inv_freq, positions.reshape(groups, LANES))
    for l in range(depth):
        x = _layer(x, cos_t, sin_t, w_in[l], pool_w[l], pool_scale[l], q_norm_g[l], w_q_up[l],
                   kv_norm_g[l], w_k_up[l], w_v_up[l], w_o[l], ln1_g[l], ln1_b[l],
                   w_gate[l], w_up[l], w_down[l], ln2_g[l], ln2_b[l], alpha)
    return x
```

```python
import functools
import math

import jax
import jax.numpy as jnp
from jax import lax
from jax.experimental import pallas as pl
from jax.experimental.pallas import tpu as pltpu

D_MODEL = 1024
POOL_WIDTH = 512
POOL_WINDOWS = (2, 4, 8, 16)
POOL_GROUP_DIM = 128
N_HEADS = 8
QK_NOPE_DIM = 64
QK_ROPE_DIM = 32
V_HEAD_DIM = 64
QK_HEAD_DIM = QK_NOPE_DIM + QK_ROPE_DIM
ATTN_WIDTH = N_HEADS * V_HEAD_DIM
Q_LORA_RANK = 384
KV_LORA_RANK = 256
ROPE_THETA = 10000.0
FFN_HIDDEN = 2816
LN_EPS = 1e-5
RMS_EPS = 1e-6

LANES = 128
SUBLANES = 8
BF16_ROWS = 2 * SUBLANES
MXU_COLS = 256
V_SLOT = V_HEAD_DIM + BF16_ROWS
VT_ROWS = N_HEADS * V_SLOT
HEAD_SLOT = LANES
ROPE_LO = QK_NOPE_DIM
ROPE_HALF = QK_ROPE_DIM // 2
H_WIDTH = POOL_WIDTH + Q_LORA_RANK + KV_LORA_RANK + LANES
POOL_HALO = 8
VMEM_LIMIT = 56 * 1024 * 1024

PROJ_TILE = 1024
ATTN_Q_TILE = 512
ATTN_KEY_CHUNK = 256
ATTN_HEADS_PER_STEP = 8
ATTN_SKEW = 3
FFN_TILE = 512
FFN_SUBTILE = 256
assert FFN_TILE % FFN_SUBTILE == 0
FFN_CHUNKS = ((0, 768), (768, 1536), (1536, 2304), (2304, FFN_HIDDEN))
assert all(lo % MXU_COLS == 0 for lo, _ in FFN_CHUNKS)

BF16 = jnp.bfloat16
F32 = jnp.float32


def _rope(t, cos, sin_signed):
    partner = pltpu.roll(t, LANES - ROPE_HALF, 1)
    return t * cos + partner * sin_signed


def _rms(t, g):
    y = t * lax.rsqrt(jnp.mean(t * t, axis=-1, keepdims=True) + RMS_EPS)
    return y * g


def _layer_norm(y, g, b):
    mu = jnp.mean(y, axis=-1, keepdims=True)
    d = y - mu
    var = jnp.mean(d * d, axis=-1, keepdims=True)
    return d * lax.rsqrt(var + LN_EPS) * g + b


def _rope_table_kernel(invf_ref, pos_ref, cos_ref, sin_ref):
    pos = pos_ref[...].astype(F32)
    for f in range(ROPE_HALF):
        ang = pos * invf_ref[f]
        cos_ref[:, f, :] = jnp.cos(ang)
        sin_ref[:, f, :] = jnp.sin(ang)


def _rope_lanes(t, outside):
    tm = t.shape[1]
    full = jnp.concatenate([jnp.full((ROPE_LO, tm), outside, F32), t, t,
                            jnp.zeros((LANES - ROPE_LO - QK_ROPE_DIM, tm), F32)], axis=0)
    return full.T


def _proj_kernel(x_ref, cos_ref, sin_ref, w_in_ref, qg_ref, wq_ref, kvg_ref, wk_ref, wv_ref, ones_ref,
                 u_ref, q_ref, k_ref, v_ref, *, q_scale):
    x = x_ref[0].astype(BF16)
    h = jnp.dot(x, w_in_ref[...], preferred_element_type=F32)
    u_ref[0] = h[:, :POOL_WIDTH]
    o_kv = POOL_WIDTH + Q_LORA_RANK
    o_kr = o_kv + KV_LORA_RANK
    cq = _rms(h[:, POOL_WIDTH:o_kv], qg_ref[...])
    ckv = _rms(h[:, o_kv:o_kr], kvg_ref[...])
    kr = h[:, o_kr:]

    groups = cos_ref.shape[0]
    cos_t = jnp.concatenate([cos_ref[j] for j in range(groups)], axis=1)
    sin_t = jnp.concatenate([sin_ref[j] for j in range(groups)], axis=1)
    cos = _rope_lanes(cos_t, 1.0)
    sin = _rope_lanes(sin_t, 0.0)
    first_half = lax.broadcasted_iota(jnp.int32, cos.shape, 1) < ROPE_LO + ROPE_HALF
    sin = jnp.where(first_half, -sin, sin)

    cq_b = cq.astype(BF16)
    ckv_b = ckv.astype(BF16)
    nt_dims = (((1,), (1,)), ((), ()))
    qt = lax.dot_general(wq_ref[...], cq_b, nt_dims, preferred_element_type=F32)
    k_nope = jnp.dot(ckv_b, wk_ref[...], preferred_element_type=F32)
    vt = lax.dot_general(wv_ref[...], ckv_b, nt_dims, preferred_element_type=F32)
    v_ref[0] = (vt + ones_ref[...]).astype(BF16)

    cos_q = cos_t * q_scale
    sin_q = sin_t * q_scale
    k_rope = _rope(kr, cos, sin)
    for hd in range(N_HEADS):
        sl = slice(hd * HEAD_SLOT, (hd + 1) * HEAD_SLOT)
        k_ref[0, :, sl] = (k_nope[:, sl] + k_rope).astype(BF16)
        base = hd * HEAD_SLOT
        r1 = qt[base + ROPE_LO:base + ROPE_LO + ROPE_HALF]
        r2 = qt[base + ROPE_LO + ROPE_HALF:base + ROPE_LO + QK_ROPE_DIM]
        q_ref[0, base:base + ROPE_LO, :] = (qt[base:base + ROPE_LO] * q_scale).astype(BF16)
        q_ref[0, base + ROPE_LO:base + HEAD_SLOT, :] = jnp.concatenate(
            [r1 * cos_q - r2 * sin_q, r2 * cos_q + r1 * sin_q,
             jnp.zeros((HEAD_SLOT - QK_HEAD_DIM, qt.shape[1]), F32)], axis=0).astype(BF16)


def _row_groups(t, op):
    return op(t.reshape(t.shape[0] // SUBLANES, SUBLANES, t.shape[1]), axis=0)


def _attn_kernel(q_ref, qn_ref, k_ref, vt_ref, zero_ref, w32_a, w32_b, w32_c,
                 o_ref, w16_a, w16_b, w16_c, s_ref, mx_ref, *, kc):
    seq = k_ref.shape[1]
    hp = k_ref.shape[2] // HEAD_SLOT
    n_chunks = seq // kc
    assert hp % 2 == 0

    def score_chunk(q, h, c):
        hs = slice(h * HEAD_SLOT, (h + 1) * HEAD_SLOT)
        ks = slice(c * kc, (c + 1) * kc)
        s = jnp.dot(k_ref[0, ks, hs], q[0, hs, :], preferred_element_type=F32)
        s_ref[h % 2, ks, :] = s
        return _row_groups(s, jnp.max)

    def fold(cms):
        mx = cms[0]
        for cm in cms[1:]:
            mx = jnp.maximum(mx, cm)
        return mx

    @pl.when(pl.program_id(2) == 0)
    def _():
        mx_ref[...] = fold([score_chunk(q_ref, 0, c) for c in range(n_chunks)])

    mx = mx_ref[...]
    zero_bits = zero_ref[0:1, :]
    outs = []
    for h in range(hp):
        m = jnp.max(mx, axis=0, keepdims=True)
        ds = slice(h * V_SLOT, (h + 1) * V_SLOT)
        cms = []
        acc = None
        for c in range(n_chunks):
            ks = slice(c * kc, (c + 1) * kc)
            cms.append(score_chunk(q_ref, h + 1, c) if h + 1 < hp else score_chunk(qn_ref, 0, c))
            m_c = m
            if c >= ATTN_SKEW:
                dep = lax.bitcast_convert_type(cms[c - ATTN_SKEW][0:1, :], jnp.int32) & zero_bits
                m_c = m + lax.bitcast_convert_type(dep, F32)
            p = jnp.exp2(s_ref[h % 2, ks, :] - m_c)
            o = jnp.dot(vt_ref[0, ds, ks], p.astype(BF16), preferred_element_type=F32)
            acc = o if acc is None else acc + o
        outs.append(acc[:V_HEAD_DIM] / acc[V_HEAD_DIM:V_HEAD_DIM + 1])
        mx = fold(cms)
    mx_ref[...] = mx
    o_ref[0] = jnp.concatenate(outs, axis=0).T.astype(BF16)
    for w32, w16 in ((w32_a, w16_a), (w32_b, w16_b), (w32_c, w16_c)):
        w16[...] = w32[...].astype(BF16)


def _fill_halo_buffer(i, n_tiles, u_ref, uprev_ref, unext_ref, ext_ref):
    tm = u_ref.shape[0]
    ext_ref[0:POOL_HALO] = jnp.where(i > 0, uprev_ref[...], 0.0)
    ext_ref[POOL_HALO:POOL_HALO + tm] = u_ref[...]
    ext_ref[POOL_HALO + tm:] = jnp.where(i < n_tiles - 1, unext_ref[...], 0.0)


def _mix_stages(emit, seq_row0, r0, th, seq, alpha, attn_ref, x_ref,
                pw_ref, ps_ref, wo_ref, g_ref, b_ref, ext_ref):
    row = seq_row0 + r0 + lax.broadcasted_iota(jnp.int32, (th, 1), 0)
    mixed = []
    for g, w in enumerate(POOL_WINDOWS):
        cols = slice(g * POOL_GROUP_DIM, (g + 1) * POOL_GROUP_DIM)
        back = w // 2
        win = ext_ref[pl.ds(POOL_HALO + r0 - back, th), cols]
        for d in range(1 - back, w - back):
            win = win + ext_ref[pl.ds(POOL_HALO + r0 + d, th), cols]
        cnt = (jnp.minimum(row + (w - back), seq) - jnp.maximum(row - back, 0)).astype(F32)
        pg = win / cnt - ext_ref[pl.ds(POOL_HALO + r0, th), cols]
        og = jnp.dot(pg.astype(BF16), pw_ref[g], preferred_element_type=F32) * ps_ref[:, cols]
        mixed.append(og.astype(BF16))
        yield
    mixed.append(attn_ref[r0:r0 + th, :])
    acc = jnp.dot(jnp.concatenate(mixed, axis=1), wo_ref[...], preferred_element_type=F32)
    y = alpha * x_ref[r0:r0 + th, :] + acc
    emit(_layer_norm(y, g_ref[...], b_ref[...]))
    yield


def _ffn_stages(emit, x, alpha, wg_ref, wu_ref, wd_ref, g_ref, b_ref):
    xb = x.astype(BF16)
    ffn = None
    for lo, hi in FFN_CHUNKS:
        gate = jnp.dot(xb, wg_ref[:, lo:hi], preferred_element_type=F32)
        up = jnp.dot(xb, wu_ref[:, lo:hi], preferred_element_type=F32)
        hid = gate * (1.0 / (1.0 + jnp.exp(-gate))) * up
        part = jnp.dot(hid.astype(BF16), wd_ref[lo:hi, :], preferred_element_type=F32)
        ffn = part if ffn is None else ffn + part
        yield
    y = alpha * x + ffn
    emit(_layer_norm(y, g_ref[...], b_ref[...]))
    yield


def _mixffn_kernel(u_ref, uprev_ref, unext_ref, attn_ref, x_ref,
                   u0_ref, unext0_ref, attn0_ref, x0_ref,
                   pw_ref, ps_ref, wo_ref, g1_ref, b1_ref, wg_ref, wu_ref, wd_ref, g2_ref, b2_ref,
                   out_ref, ext_ref, x1_ref, *, seq, alpha):
    s = pl.program_id(0)
    n_steps = pl.num_programs(0)
    tm = u_ref.shape[0]
    n_tiles = seq // tm
    th = FFN_SUBTILE
    mix_w = (pw_ref, ps_ref, wo_ref, g1_ref, b1_ref, ext_ref)

    def store(ref, slot, r0):
        def emit(v):
            ref[slot, r0:r0 + th, :] = v
        return emit

    def store_out(r0):
        def emit(v):
            out_ref[r0:r0 + th, :] = v
        return emit

    @pl.when(s == 0)
    def _():
        _fill_halo_buffer(0, n_tiles, u0_ref, unext0_ref, unext0_ref, ext_ref)
        for r0 in range(0, tm, th):
            for _ in _mix_stages(store(x1_ref, 0, r0), 0, r0, th, seq, alpha, attn0_ref, x0_ref, *mix_w):
                pass

    nxt = jnp.minimum(s + 1, n_steps - 1)
    x_cur = [x1_ref[s % 2, r0:r0 + th, :] for r0 in range(0, tm, th)]
    _fill_halo_buffer(nxt % n_tiles, n_tiles, u_ref, uprev_ref, unext_ref, ext_ref)
    for j, r0 in enumerate(range(0, tm, th)):
        ffn = _ffn_stages(store_out(r0), x_cur[j], alpha, wg_ref, wu_ref, wd_ref, g2_ref, b2_ref)
        mix = _mix_stages(store(x1_ref, (s + 1) % 2, r0), (nxt % n_tiles) * tm, r0, th, seq, alpha,
                          attn_ref, x_ref, *mix_w)
        for _ in range(len(FFN_CHUNKS) + 1):
            next(ffn)
            next(mix)
        assert next(mix, None) is None and next(ffn, None) is None


def _full(shape):
    return pl.BlockSpec(shape, lambda *_: (0,) * len(shape))


def _params(sem):
    return pltpu.CompilerParams(dimension_semantics=sem, vmem_limit_bytes=VMEM_LIMIT)


def _pad_heads(w, dim):
    r = w.shape[0]
    w = w.reshape(r, N_HEADS, dim)
    return jnp.pad(w, ((0, 0), (0, 0), (0, HEAD_SLOT - dim))).reshape(r, N_HEADS * HEAD_SLOT)


def _rope_slot(w):
    return jnp.concatenate([w, w[..., :ROPE_HALF], jnp.zeros_like(w[..., :ROPE_HALF])], axis=-1)


def _layer(x, cos_t, sin_t, w_in, pool_w, pool_scale, q_norm_g, w_q_up, kv_norm_g, w_k_up, w_v_up,
           w_o, ln1_g, ln1_b, w_gate, w_up, w_down, ln2_g, ln2_b, alpha):
    bsz, seq, _ = x.shape
    tm, tq, tf, hp = PROJ_TILE, ATTN_Q_TILE, FFN_TILE, ATTN_HEADS_PER_STEP
    assert seq % tm == 0 and seq % tq == 0 and seq % tf == 0 and N_HEADS % hp == 0
    assert len(POOL_WINDOWS) == len(FFN_CHUNKS)
    o_kr = POOL_WIDTH + Q_LORA_RANK + KV_LORA_RANK

    w_in_ext = jnp.concatenate(
        [w_in[:, :o_kr], jnp.zeros((D_MODEL, ROPE_LO), w_in.dtype), _rope_slot(w_in[:, o_kr:])],
        axis=1).astype(BF16)
    wq = _pad_heads(w_q_up, QK_HEAD_DIM).T.astype(BF16)
    wk = _pad_heads(w_k_up, QK_NOPE_DIM).astype(BF16)
    wv = jnp.pad(w_v_up.T.reshape(N_HEADS, V_HEAD_DIM, KV_LORA_RANK),
                 ((0, 0), (0, V_SLOT - V_HEAD_DIM), (0, 0))).reshape(VT_ROWS, KV_LORA_RANK).astype(BF16)
    ones_col = (jnp.arange(VT_ROWS) % V_SLOT == V_HEAD_DIM).astype(F32)[:, None]
    q_scale = (QK_HEAD_DIM ** -0.5) * math.log2(math.e)

    tok = lambda w: pl.BlockSpec((1, tm, w), lambda b, i: (b, i, 0))
    rope = lambda: pl.BlockSpec((tm // LANES, ROPE_HALF, LANES),
                                lambda b, i: (b * (seq // tm) + i, 0, 0))
    u, q, k, v = pl.pallas_call(
        functools.partial(_proj_kernel, q_scale=q_scale),
        grid=(bsz, seq // tm),
        in_specs=[tok(D_MODEL), rope(), rope(), _full((D_MODEL, H_WIDTH)),
                  _full((1, Q_LORA_RANK)), _full((N_HEADS * HEAD_SLOT, Q_LORA_RANK)),
                  _full((1, KV_LORA_RANK)), _full((KV_LORA_RANK, N_HEADS * HEAD_SLOT)),
                  _full((VT_ROWS, KV_LORA_RANK)), _full((VT_ROWS, 1))],
        out_specs=[tok(POOL_WIDTH),
                   pl.BlockSpec((1, N_HEADS * HEAD_SLOT, tm), lambda b, i: (b, 0, i)),
                   tok(N_HEADS * HEAD_SLOT),
                   pl.BlockSpec((1, VT_ROWS, tm), lambda b, i: (b, 0, i))],
        out_shape=[jax.ShapeDtypeStruct((bsz, seq, POOL_WIDTH), F32),
                   jax.ShapeDtypeStruct((bsz, N_HEADS * HEAD_SLOT, seq), BF16),
                   jax.ShapeDtypeStruct((bsz, seq, N_HEADS * HEAD_SLOT), BF16),
                   jax.ShapeDtypeStruct((bsz, VT_ROWS, seq), BF16)],
        compiler_params=_params(("parallel", "parallel")),
        name="proj",
    )(x, cos_t, sin_t, w_in_ext, q_norm_g[None, :], wq, kv_norm_g[None, :], wk, wv, ones_col)

    n_groups, n_q = N_HEADS // hp, seq // tq
    steps = bsz * n_groups * n_q
    w_rows, wd_rows = D_MODEL // steps, FFN_HIDDEN // (steps // 2)
    assert w_rows * steps == D_MODEL and wd_rows * (steps // 2) == FFN_HIDDEN
    assert w_rows % BF16_ROWS == 0 and wd_rows % BF16_ROWS == 0
    step_of = lambda b, g, i: (b * n_groups + g) * n_q + i
    w_slice = pl.BlockSpec((w_rows, FFN_HIDDEN), lambda b, g, i: (step_of(b, g, i), 0))
    wd_slice = pl.BlockSpec((wd_rows, D_MODEL), lambda b, g, i: (step_of(b, g, i) // 2, 0))
    attn, wg16, wu16, wd16 = pl.pallas_call(
        functools.partial(_attn_kernel, kc=ATTN_KEY_CHUNK),
        grid=(bsz, n_groups, n_q),
        in_specs=[pl.BlockSpec((1, hp * HEAD_SLOT, tq), lambda b, g, i: (b, g, i)),
                  pl.BlockSpec((1, hp * HEAD_SLOT, tq),
                               lambda b, g, i: (b, g, jnp.minimum(i + 1, n_q - 1))),
                  pl.BlockSpec((1, seq, hp * HEAD_SLOT), lambda b, g, i: (b, 0, g)),
                  pl.BlockSpec((1, hp * V_SLOT, seq), lambda b, g, i: (b, g, 0)),
                  _full((SUBLANES, tq)), w_slice, w_slice, wd_slice],
        out_specs=[pl.BlockSpec((1, tq, hp * V_HEAD_DIM), lambda b, g, i: (b, i, g)),
                   w_slice, w_slice, wd_slice],
        out_shape=[jax.ShapeDtypeStruct((bsz, seq, ATTN_WIDTH), BF16)]
                  + [jax.ShapeDtypeStruct(w.shape, BF16) for w in (w_gate, w_up, w_down)],
        scratch_shapes=[pltpu.VMEM((2, seq, tq), F32),
                        pltpu.VMEM((SUBLANES, tq), F32)],
        compiler_params=_params(("arbitrary", "arbitrary", "arbitrary")),
        name="attn",
    )(q, q, k, v, jnp.zeros((SUBLANES, tq), jnp.int32), w_gate, w_up, w_down)

    rows = bsz * seq
    n_steps = rows // tf
    halo_blocks = tf // POOL_HALO
    last_halo = rows // POOL_HALO - 1
    ahead = lambda s: jnp.minimum(s + 1, n_steps - 1)
    row_blk = lambda w, idx, **kw: pl.BlockSpec((tf, w), lambda s: (idx(s), 0), **kw)
    halo = lambda idx, **kw: pl.BlockSpec((POOL_HALO, POOL_WIDTH), lambda s: (idx(s), 0), **kw)
    first = lambda s: 0
    single = dict(pipeline_mode=pl.Buffered(1))
    once = lambda shape: pl.BlockSpec(shape, lambda s: (0,) * len(shape), **single)
    u2 = u.reshape(rows, POOL_WIDTH)
    attn2 = attn.reshape(rows, ATTN_WIDTH)
    xr = x.reshape(rows, D_MODEL)
    x2 = pl.pallas_call(
        functools.partial(_mixffn_kernel, seq=seq, alpha=alpha),
        grid=(n_steps,),
        in_specs=[row_blk(POOL_WIDTH, ahead),
                  halo(lambda s: jnp.maximum(ahead(s) * halo_blocks - 1, 0)),
                  halo(lambda s: jnp.minimum((ahead(s) + 1) * halo_blocks, last_halo)),
                  row_blk(ATTN_WIDTH, ahead), row_blk(D_MODEL, ahead),
                  row_blk(POOL_WIDTH, first, **single), halo(lambda s: halo_blocks, **single),
                  row_blk(ATTN_WIDTH, first, **single), row_blk(D_MODEL, first, **single),
                  once((len(POOL_WINDOWS), POOL_GROUP_DIM, POOL_GROUP_DIM)), once((1, POOL_WIDTH)),
                  once((D_MODEL, D_MODEL)), once((1, D_MODEL)), once((1, D_MODEL)),
                  once((D_MODEL, FFN_HIDDEN)), once((D_MODEL, FFN_HIDDEN)), once((FFN_HIDDEN, D_MODEL)),
                  once((1, D_MODEL)), once((1, D_MODEL))],
        out_specs=pl.BlockSpec((tf, D_MODEL), lambda s: (s, 0)),
        out_shape=jax.ShapeDtypeStruct((rows, D_MODEL), F32),
        scratch_shapes=[pltpu.VMEM((tf + 2 * POOL_HALO, POOL_WIDTH), F32),
                        pltpu.VMEM((2, tf, D_MODEL), F32)],
        compiler_params=_params(("arbitrary",)),
        name="mixffn",
    )(u2, u2, u2, attn2, xr, u2, u2, attn2, xr,
      pool_w.astype(BF16), pool_scale[None, :], w_o.astype(BF16), ln1_g[None, :], ln1_b[None, :],
      wg16, wu16, wd16, ln2_g[None, :], ln2_b[None, :])
    return x2.reshape(bsz, seq, D_MODEL)


def kernel(x, positions, w_in, pool_w, pool_scale, q_norm_g, w_q_up, kv_norm_g, w_k_up, w_v_up, w_o, ln1_g, ln1_b, w_gate, w_up, w_down, ln2_g, ln2_b):
    depth = w_in.shape[0]
    alpha = (2.0 * depth) ** 0.25
    inv_freq = 1.0 / (ROPE_THETA ** (jnp.arange(0, QK_ROPE_DIM, 2, dtype=F32) / QK_ROPE_DIM))
    tokens = positions.size
    groups = tokens // LANES
    table = (groups, ROPE_HALF, LANES)
    cos_t, sin_t = pl.pallas_call(
        _rope_table_kernel,
        in_specs=[pl.BlockSpec(memory_space=pltpu.SMEM), _full((groups, LANES))],
        out_specs=[_full(table), _full(table)],
        out_shape=[jax.ShapeDtypeStruct(table, F32)] * 2,
        grid=(1,),
        name="rope_table",
    )(inv_freq, positions.reshape(groups, LANES))
    for l in range(depth):
        x = _layer(x, cos_t, sin_t, w_in[l], pool_w[l], pool_scale[l], q_norm_g[l], w_q_up[l],
                   kv_norm_g[l], w_k_up[l], w_v_up[l], w_o[l], ln1_g[l], ln1_b[l],
                   w_gate[l], w_up[l], w_down[l], ln2_g[l], ln2_b[l], alpha)
    return x
```

```python
import functools
import math

import jax
import jax.numpy as jnp
from jax import lax
from jax.experimental import pallas as pl
from jax.experimental.pallas import tpu as pltpu

D_MODEL = 1024
POOL_WIDTH = 512
POOL_WINDOWS = (2, 4, 8, 16)
POOL_GROUP_DIM = 128
N_HEADS = 8
QK_NOPE_DIM = 64
QK_ROPE_DIM = 32
V_HEAD_DIM = 64
QK_HEAD_DIM = QK_NOPE_DIM + QK_ROPE_DIM
ATTN_WIDTH = N_HEADS * V_HEAD_DIM
Q_LORA_RANK = 384
KV_LORA_RANK = 256
ROPE_THETA = 10000.0
FFN_HIDDEN = 2816
LN_EPS = 1e-5
RMS_EPS = 1e-6

LANES = 128
SUBLANES = 8
BF16_ROWS = 2 * SUBLANES
MXU_COLS = 256
V_SLOT = V_HEAD_DIM + BF16_ROWS
VT_ROWS = N_HEADS * V_SLOT
HEAD_SLOT = LANES
ROPE_LO = QK_NOPE_DIM
ROPE_HALF = QK_ROPE_DIM // 2
H_WIDTH = POOL_WIDTH + Q_LORA_RANK + KV_LORA_RANK + LANES
POOL_HALO = 8
VMEM_LIMIT = 56 * 1024 * 1024

PROJ_TILE = 1024
ATTN_Q_TILE = 512
ATTN_KEY_CHUNK = 256
ATTN_HEADS_PER_STEP = 8
ATTN_SKEW = 3
FFN_TILE = 512
FFN_SUBTILE = 256
assert FFN_TILE % FFN_SUBTILE == 0
FFN_CHUNKS = ((0, 768), (768, 1536), (1536, 2304), (2304, FFN_HIDDEN))
assert all(lo % MXU_COLS == 0 for lo, _ in FFN_CHUNKS)

BF16 = jnp.bfloat16
F32 = jnp.float32


def _rope(t, cos, sin_signed):
    partner = pltpu.roll(t, LANES - ROPE_HALF, 1)
    return t * cos + partner * sin_signed


def _rms(t, g):
    y = t * lax.rsqrt(jnp.mean(t * t, axis=-1, keepdims=True) + RMS_EPS)
    return y * g


def _layer_norm(y, g, b):
    mu = jnp.mean(y, axis=-1, keepdims=True)
    d = y - mu
    var = jnp.mean(d * d, axis=-1, keepdims=True)
    return d * lax.rsqrt(var + LN_EPS) * g + b


def _rope_table_kernel(invf_ref, pos_ref, cos_ref, sin_ref):
    pos = pos_ref[...].astype(F32)
    for f in range(ROPE_HALF):
        ang = pos * invf_ref[f]
        cos_ref[:, f, :] = jnp.cos(ang)
        sin_ref[:, f, :] = jnp.sin(ang)


def _rope_lanes(t, outside):
    tm = t.shape[1]
    full = jnp.concatenate([jnp.full((ROPE_LO, tm), outside, F32), t, t,
                            jnp.zeros((LANES - ROPE_LO - QK_ROPE_DIM, tm), F32)], axis=0)
    return full.T


def _proj_kernel(x_ref, cos_ref, sin_ref, w_in_ref, qg_ref, wq_ref, kvg_ref, wk_ref, wv_ref, ones_ref,
                 u_ref, q_ref, k_ref, v_ref, *, q_scale):
    x = x_ref[0].astype(BF16)
    h = jnp.dot(x, w_in_ref[...], preferred_element_type=F32)
    u_ref[0] = h[:, :POOL_WIDTH]
    o_kv = POOL_WIDTH + Q_LORA_RANK
    o_kr = o_kv + KV_LORA_RANK
    cq = _rms(h[:, POOL_WIDTH:o_kv], qg_ref[...])
    ckv = _rms(h[:, o_kv:o_kr], kvg_ref[...])
    kr = h[:, o_kr:]

    groups = cos_ref.shape[0]
    cos_t = jnp.concatenate([cos_ref[j] for j in range(groups)], axis=1)
    sin_t = jnp.concatenate([sin_ref[j] for j in range(groups)], axis=1)
    cos = _rope_lanes(cos_t, 1.0)
    sin = _rope_lanes(sin_t, 0.0)
    first_half = lax.broadcasted_iota(jnp.int32, cos.shape, 1) < ROPE_LO + ROPE_HALF
    sin = jnp.where(first_half, -sin, sin)

    cq_b = cq.astype(BF16)
    ckv_b = ckv.astype(BF16)
    nt_dims = (((1,), (1,)), ((), ()))
    qt = lax.dot_general(wq_ref[...], cq_b, nt_dims, preferred_element_type=F32)
    k_nope = jnp.dot(ckv_b, wk_ref[...], preferred_element_type=F32)
    vt = lax.dot_general(wv_ref[...], ckv_b, nt_dims, preferred_element_type=F32)
    v_ref[0] = (vt + ones_ref[...]).astype(BF16)

    cos_q = cos_t * q_scale
    sin_q = sin_t * q_scale
    k_rope = _rope(kr, cos, sin)
    for hd in range(N_HEADS):
        sl = slice(hd * HEAD_SLOT, (hd + 1) * HEAD_SLOT)
        k_ref[0, :, sl] = (k_nope[:, sl] + k_rope).astype(BF16)
        base = hd * HEAD_SLOT
        r1 = qt[base + ROPE_LO:base + ROPE_LO + ROPE_HALF]
        r2 = qt[base + ROPE_LO + ROPE_HALF:base + ROPE_LO + QK_ROPE_DIM]
        q_ref[0, base:base + ROPE_LO, :] = (qt[base:base + ROPE_LO] * q_scale).astype(BF16)
        q_ref[0, base + ROPE_LO:base + HEAD_SLOT, :] = jnp.concatenate(
            [r1 * cos_q - r2 * sin_q, r2 * cos_q + r1 * sin_q,
             jnp.zeros((HEAD_SLOT - QK_HEAD_DIM, qt.shape[1]), F32)], axis=0).astype(BF16)


def _row_groups(t, op):
    return op(t.reshape(t.shape[0] // SUBLANES, SUBLANES, t.shape[1]), axis=0)


def _attn_kernel(q_ref, qn_ref, k_ref, kn_ref, vt_ref, zero_ref, w32_a, w32_b, w32_c,
                 o_ref, w16_a, w16_b, w16_c, s_ref, mx_ref, *, kc):
    seq = k_ref.shape[1]
    hp = k_ref.shape[2] // HEAD_SLOT
    n_chunks = seq // kc
    assert hp % 2 == 0

    last_tile = pl.program_id(2) == pl.num_programs(2) - 1

    def score_chunk(q, h, c, next_tile=False):
        hs = slice(h * HEAD_SLOT, (h + 1) * HEAD_SLOT)
        ks = slice(c * kc, (c + 1) * kc)
        keys = k_ref[0, ks, hs]
        if next_tile:
            keys = jnp.where(last_tile, kn_ref[0, ks, :], keys)
        s = jnp.dot(keys, q[0, hs, :], preferred_element_type=F32)
        s_ref[h % 2, ks, :] = s
        return _row_groups(s, jnp.max)

    def fold(cms):
        mx = cms[0]
        for cm in cms[1:]:
            mx = jnp.maximum(mx, cm)
        return mx

    @pl.when((pl.program_id(0) == 0) & (pl.program_id(2) == 0))
    def _():
        mx_ref[...] = fold([score_chunk(q_ref, 0, c) for c in range(n_chunks)])

    mx = mx_ref[...]
    zero_bits = zero_ref[0:1, :]
    outs = []
    for h in range(hp):
        m = jnp.max(mx, axis=0, keepdims=True)
        ds = slice(h * V_SLOT, (h + 1) * V_SLOT)
        cms = []
        acc = None
        for c in range(n_chunks):
            ks = slice(c * kc, (c + 1) * kc)
            cms.append(score_chunk(q_ref, h + 1, c) if h + 1 < hp
                       else score_chunk(qn_ref, 0, c, next_tile=True))
            m_c = m
            if c >= ATTN_SKEW:
                dep = lax.bitcast_convert_type(cms[c - ATTN_SKEW][0:1, :], jnp.int32) & zero_bits
                m_c = m + lax.bitcast_convert_type(dep, F32)
            p = jnp.exp2(s_ref[h % 2, ks, :] - m_c)
            o = jnp.dot(vt_ref[0, ds, ks], p.astype(BF16), preferred_element_type=F32)
            acc = o if acc is None else acc + o
        outs.append(acc[:V_HEAD_DIM] / acc[V_HEAD_DIM:V_HEAD_DIM + 1])
        mx = fold(cms)
    mx_ref[...] = mx
    o_ref[0] = jnp.concatenate(outs, axis=0).T.astype(BF16)
    for w32, w16 in ((w32_a, w16_a), (w32_b, w16_b), (w32_c, w16_c)):
        w16[...] = w32[...].astype(BF16)


def _fill_halo_buffer(i, n_tiles, u_ref, uprev_ref, unext_ref, ext_ref):
    tm = u_ref.shape[0]
    ext_ref[0:POOL_HALO] = jnp.where(i > 0, uprev_ref[...], 0.0)
    ext_ref[POOL_HALO:POOL_HALO + tm] = u_ref[...]
    ext_ref[POOL_HALO + tm:] = jnp.where(i < n_tiles - 1, unext_ref[...], 0.0)


def _mix_stages(emit, seq_row0, r0, th, seq, alpha, attn_ref, x_ref,
                pw_ref, ps_ref, wo_ref, g_ref, b_ref, ext_ref):
    row = seq_row0 + r0 + lax.broadcasted_iota(jnp.int32, (th, 1), 0)
    mixed = []
    for g, w in enumerate(POOL_WINDOWS):
        cols = slice(g * POOL_GROUP_DIM, (g + 1) * POOL_GROUP_DIM)
        back = w // 2
        win = ext_ref[pl.ds(POOL_HALO + r0 - back, th), cols]
        for d in range(1 - back, w - back):
            win = win + ext_ref[pl.ds(POOL_HALO + r0 + d, th), cols]
        cnt = (jnp.minimum(row + (w - back), seq) - jnp.maximum(row - back, 0)).astype(F32)
        pg = win / cnt - ext_ref[pl.ds(POOL_HALO + r0, th), cols]
        og = jnp.dot(pg.astype(BF16), pw_ref[g], preferred_element_type=F32) * ps_ref[:, cols]
        mixed.append(og.astype(BF16))
        yield
    mixed.append(attn_ref[r0:r0 + th, :])
    acc = jnp.dot(jnp.concatenate(mixed, axis=1), wo_ref[...], preferred_element_type=F32)
    y = alpha * x_ref[r0:r0 + th, :] + acc
    emit(_layer_norm(y, g_ref[...], b_ref[...]))
    yield


def _ffn_stages(emit, x, alpha, wg_ref, wu_ref, wd_ref, g_ref, b_ref):
    xb = x.astype(BF16)
    ffn = None
    for lo, hi in FFN_CHUNKS:
        gate = jnp.dot(xb, wg_ref[:, lo:hi], preferred_element_type=F32)
        up = jnp.dot(xb, wu_ref[:, lo:hi], preferred_element_type=F32)
        hid = gate * (1.0 / (1.0 + jnp.exp(-gate))) * up
        part = jnp.dot(hid.astype(BF16), wd_ref[lo:hi, :], preferred_element_type=F32)
        ffn = part if ffn is None else ffn + part
        yield
    y = alpha * x + ffn
    emit(_layer_norm(y, g_ref[...], b_ref[...]))
    yield


def _mixffn_kernel(u_ref, uprev_ref, unext_ref, attn_ref, x_ref,
                   u0_ref, unext0_ref, attn0_ref, x0_ref,
                   pw_ref, ps_ref, wo_ref, g1_ref, b1_ref, wg_ref, wu_ref, wd_ref, g2_ref, b2_ref,
                   out_ref, ext_ref, x1_ref, *, seq, alpha):
    s = pl.program_id(0)
    n_steps = pl.num_programs(0)
    tm = u_ref.shape[0]
    n_tiles = seq // tm
    th = FFN_SUBTILE
    mix_w = (pw_ref, ps_ref, wo_ref, g1_ref, b1_ref, ext_ref)

    def store(ref, slot, r0):
        def emit(v):
            ref[slot, r0:r0 + th, :] = v
        return emit

    def store_out(r0):
        def emit(v):
            out_ref[r0:r0 + th, :] = v
        return emit

    @pl.when(s == 0)
    def _():
        _fill_halo_buffer(0, n_tiles, u0_ref, unext0_ref, unext0_ref, ext_ref)
        for r0 in range(0, tm, th):
            for _ in _mix_stages(store(x1_ref, 0, r0), 0, r0, th, seq, alpha, attn0_ref, x0_ref, *mix_w):
                pass

    nxt = jnp.minimum(s + 1, n_steps - 1)
    x_cur = [x1_ref[s % 2, r0:r0 + th, :] for r0 in range(0, tm, th)]
    _fill_halo_buffer(nxt % n_tiles, n_tiles, u_ref, uprev_ref, unext_ref, ext_ref)
    for j, r0 in enumerate(range(0, tm, th)):
        ffn = _ffn_stages(store_out(r0), x_cur[j], alpha, wg_ref, wu_ref, wd_ref, g2_ref, b2_ref)
        mix = _mix_stages(store(x1_ref, (s + 1) % 2, r0), (nxt % n_tiles) * tm, r0, th, seq, alpha,
                          attn_ref, x_ref, *mix_w)
        for _ in range(len(FFN_CHUNKS) + 1):
            next(ffn)
            next(mix)
        assert next(mix, None) is None and next(ffn, None) is None


def _full(shape):
    return pl.BlockSpec(shape, lambda *_: (0,) * len(shape))


def _params(sem):
    return pltpu.CompilerParams(dimension_semantics=sem, vmem_limit_bytes=VMEM_LIMIT)


def _pad_heads(w, dim):
    r = w.shape[0]
    w = w.reshape(r, N_HEADS, dim)
    return jnp.pad(w, ((0, 0), (0, 0), (0, HEAD_SLOT - dim))).reshape(r, N_HEADS * HEAD_SLOT)


def _rope_slot(w):
    return jnp.concatenate([w, w[..., :ROPE_HALF], jnp.zeros_like(w[..., :ROPE_HALF])], axis=-1)


def _layer(x, cos_t, sin_t, w_in, pool_w, pool_scale, q_norm_g, w_q_up, kv_norm_g, w_k_up, w_v_up,
           w_o, ln1_g, ln1_b, w_gate, w_up, w_down, ln2_g, ln2_b, alpha):
    bsz, seq, _ = x.shape
    tm, tq, tf, hp = PROJ_TILE, ATTN_Q_TILE, FFN_TILE, ATTN_HEADS_PER_STEP
    assert seq % tm == 0 and seq % tq == 0 and seq % tf == 0 and N_HEADS % hp == 0
    assert len(POOL_WINDOWS) == len(FFN_CHUNKS)
    o_kr = POOL_WIDTH + Q_LORA_RANK + KV_LORA_RANK

    w_in_ext = jnp.concatenate(
        [w_in[:, :o_kr], jnp.zeros((D_MODEL, ROPE_LO), w_in.dtype), _rope_slot(w_in[:, o_kr:])],
        axis=1).astype(BF16)
    wq = _pad_heads(w_q_up, QK_HEAD_DIM).T.astype(BF16)
    wk = _pad_heads(w_k_up, QK_NOPE_DIM).astype(BF16)
    wv = jnp.pad(w_v_up.T.reshape(N_HEADS, V_HEAD_DIM, KV_LORA_RANK),
                 ((0, 0), (0, V_SLOT - V_HEAD_DIM), (0, 0))).reshape(VT_ROWS, KV_LORA_RANK).astype(BF16)
    ones_col = (jnp.arange(VT_ROWS) % V_SLOT == V_HEAD_DIM).astype(F32)[:, None]
    q_scale = (QK_HEAD_DIM ** -0.5) * math.log2(math.e)

    tok = lambda w: pl.BlockSpec((1, tm, w), lambda b, i: (b, i, 0))
    rope = lambda: pl.BlockSpec((tm // LANES, ROPE_HALF, LANES),
                                lambda b, i: (b * (seq // tm) + i, 0, 0))
    u, q, k, v = pl.pallas_call(
        functools.partial(_proj_kernel, q_scale=q_scale),
        grid=(bsz, seq // tm),
        in_specs=[tok(D_MODEL), rope(), rope(), _full((D_MODEL, H_WIDTH)),
                  _full((1, Q_LORA_RANK)), _full((N_HEADS * HEAD_SLOT, Q_LORA_RANK)),
                  _full((1, KV_LORA_RANK)), _full((KV_LORA_RANK, N_HEADS * HEAD_SLOT)),
                  _full((VT_ROWS, KV_LORA_RANK)), _full((VT_ROWS, 1))],
        out_specs=[tok(POOL_WIDTH),
                   pl.BlockSpec((1, N_HEADS * HEAD_SLOT, tm), lambda b, i: (b, 0, i)),
                   tok(N_HEADS * HEAD_SLOT),
                   pl.BlockSpec((1, VT_ROWS, tm), lambda b, i: (b, 0, i))],
        out_shape=[jax.ShapeDtypeStruct((bsz, seq, POOL_WIDTH), F32),
                   jax.ShapeDtypeStruct((bsz, N_HEADS * HEAD_SLOT, seq), BF16),
                   jax.ShapeDtypeStruct((bsz, seq, N_HEADS * HEAD_SLOT), BF16),
                   jax.ShapeDtypeStruct((bsz, VT_ROWS, seq), BF16)],
        compiler_params=_params(("parallel", "parallel")),
        name="proj",
    )(x, cos_t, sin_t, w_in_ext, q_norm_g[None, :], wq, kv_norm_g[None, :], wk, wv, ones_col)

    n_groups, n_q = N_HEADS // hp, seq // tq
    assert n_groups == 1

    def next_tile(b, g, i):
        wraps = (i + 1) // n_q
        more = b + wraps < bsz
        return (jnp.minimum(b + wraps, bsz - 1), g, jnp.where(more, (i + 1) % n_q, n_q - 1))

    steps = bsz * n_groups * n_q
    w_rows, wd_rows = D_MODEL // steps, FFN_HIDDEN // (steps // 2)
    assert w_rows * steps == D_MODEL and wd_rows * (steps // 2) == FFN_HIDDEN
    assert w_rows % BF16_ROWS == 0 and wd_rows % BF16_ROWS == 0
    step_of = lambda b, g, i: (b * n_groups + g) * n_q + i
    w_slice = pl.BlockSpec((w_rows, FFN_HIDDEN), lambda b, g, i: (step_of(b, g, i), 0))
    wd_slice = pl.BlockSpec((wd_rows, D_MODEL), lambda b, g, i: (step_of(b, g, i) // 2, 0))
    attn, wg16, wu16, wd16 = pl.pallas_call(
        functools.partial(_attn_kernel, kc=ATTN_KEY_CHUNK),
        grid=(bsz, n_groups, n_q),
        in_specs=[pl.BlockSpec((1, hp * HEAD_SLOT, tq), lambda b, g, i: (b, g, i)),
                  pl.BlockSpec((1, hp * HEAD_SLOT, tq), next_tile),
                  pl.BlockSpec((1, seq, hp * HEAD_SLOT), lambda b, g, i: (b, 0, g)),
                  pl.BlockSpec((1, seq, HEAD_SLOT), lambda b, g, i: (jnp.minimum(b + 1, bsz - 1), 0, 0)),
                  pl.BlockSpec((1, hp * V_SLOT, seq), lambda b, g, i: (b, g, 0)),
                  _full((SUBLANES, tq)), w_slice, w_slice, wd_slice],
        out_specs=[pl.BlockSpec((1, tq, hp * V_HEAD_DIM), lambda b, g, i: (b, i, g)),
                   w_slice, w_slice, wd_slice],
        out_shape=[jax.ShapeDtypeStruct((bsz, seq, ATTN_WIDTH), BF16)]
                  + [jax.ShapeDtypeStruct(w.shape, BF16) for w in (w_gate, w_up, w_down)],
        scratch_shapes=[pltpu.VMEM((2, seq, tq), F32),
                        pltpu.VMEM((SUBLANES, tq), F32)],
        compiler_params=_params(("arbitrary", "arbitrary", "arbitrary")),
        name="attn",
    )(q, q, k, k, v, jnp.zeros((SUBLANES, tq), jnp.int32), w_gate, w_up, w_down)

    rows = bsz * seq
    n_steps = rows // tf
    halo_blocks = tf // POOL_HALO
    last_halo = rows // POOL_HALO - 1
    ahead = lambda s: jnp.minimum(s + 1, n_steps - 1)
    row_blk = lambda w, idx, **kw: pl.BlockSpec((tf, w), lambda s: (idx(s), 0), **kw)
    halo = lambda idx, **kw: pl.BlockSpec((POOL_HALO, POOL_WIDTH), lambda s: (idx(s), 0), **kw)
    first = lambda s: 0
    single = dict(pipeline_mode=pl.Buffered(1))
    once = lambda shape: pl.BlockSpec(shape, lambda s: (0,) * len(shape), **single)
    u2 = u.reshape(rows, POOL_WIDTH)
    attn2 = attn.reshape(rows, ATTN_WIDTH)
    xr = x.reshape(rows, D_MODEL)
    x2 = pl.pallas_call(
        functools.partial(_mixffn_kernel, seq=seq, alpha=alpha),
        grid=(n_steps,),
        in_specs=[row_blk(POOL_WIDTH, ahead),
                  halo(lambda s: jnp.maximum(ahead(s) * halo_blocks - 1, 0)),
                  halo(lambda s: jnp.minimum((ahead(s) + 1) * halo_blocks, last_halo)),
                  row_blk(ATTN_WIDTH, ahead), row_blk(D_MODEL, ahead),
                  row_blk(POOL_WIDTH, first, **single), halo(lambda s: halo_blocks, **single),
                  row_blk(ATTN_WIDTH, first, **single), row_blk(D_MODEL, first, **single),
                  once((len(POOL_WINDOWS), POOL_GROUP_DIM, POOL_GROUP_DIM)), once((1, POOL_WIDTH)),
                  once((D_MODEL, D_MODEL)), once((1, D_MODEL)), once((1, D_MODEL)),
                  once((D_MODEL, FFN_HIDDEN)), once((D_MODEL, FFN_HIDDEN)), once((FFN_HIDDEN, D_MODEL)),
                  once((1, D_MODEL)), once((1, D_MODEL))],
        out_specs=pl.BlockSpec((tf, D_MODEL), lambda s: (s, 0)),
        out_shape=jax.ShapeDtypeStruct((rows, D_MODEL), F32),
        scratch_shapes=[pltpu.VMEM((tf + 2 * POOL_HALO, POOL_WIDTH), F32),
                        pltpu.VMEM((2, tf, D_MODEL), F32)],
        compiler_params=_params(("arbitrary",)),
        name="mixffn",
    )(u2, u2, u2, attn2, xr, u2, u2, attn2, xr,
      pool_w.astype(BF16), pool_scale[None, :], w_o.astype(BF16), ln1_g[None, :], ln1_b[None, :],
      wg16, wu16, wd16, ln2_g[None, :], ln2_b[None, :])
    return x2.reshape(bsz, seq, D_MODEL)


def kernel(x, positions, w_in, pool_w, pool_scale, q_norm_g, w_q_up, kv_norm_g, w_k_up, w_v_up, w_o, ln1_g, ln1_b, w_gate, w_up, w_down, ln2_g, ln2_b):
    depth = w_in.shape[0]
    alpha = (2.0 * depth) ** 0.25
    inv_freq = 1.0 / (ROPE_THETA ** (jnp.arange(0, QK_ROPE_DIM, 2, dtype=F32) / QK_ROPE_DIM))
    tokens = positions.size
    groups = tokens // LANES
    table = (groups, ROPE_HALF, LANES)
    cos_t, sin_t = pl.pallas_call(
        _rope_table_kernel,
        in_specs=[pl.BlockSpec(memory_space=pltpu.SMEM), _full((groups, LANES))],
        out_specs=[_full(table), _full(table)],
        out_shape=[jax.ShapeDtypeStruct(table, F32)] * 2,
        grid=(1,),
        name="rope_table",
    )(inv_freq, positions.reshape(groups, LANES))
    for l in range(depth):
        x = _layer(x, cos_t, sin_t, w_in[l], pool_w[l], pool_scale[l], q_norm_g[l], w_q_up[l],
                   kv_norm_g[l], w_k_up[l], w_v_up[l], w_o[l], ln1_g[l], ln1_b[l],
                   w_gate[l], w_up[l], w_down[l], ln2_g[l], ln2_b[l], alpha)
    return x
```

```python
import functools
import math

import jax
import jax.numpy as jnp
from jax import lax
from jax.experimental import pallas as pl
from jax.experimental.pallas import tpu as pltpu

D_MODEL = 1024
POOL_WIDTH = 512
POOL_WINDOWS = (2, 4, 8, 16)
POOL_GROUP_DIM = 128
N_HEADS = 8
QK_NOPE_DIM = 64
QK_ROPE_DIM = 32
V_HEAD_DIM = 64
QK_HEAD_DIM = QK_NOPE_DIM + QK_ROPE_DIM
ATTN_WIDTH = N_HEADS * V_HEAD_DIM
Q_LORA_RANK = 384
KV_LORA_RANK = 256
ROPE_THETA = 10000.0
FFN_HIDDEN = 2816
LN_EPS = 1e-5
RMS_EPS = 1e-6

LANES = 128
SUBLANES = 8
BF16_ROWS = 2 * SUBLANES
MXU_COLS = 256
V_SLOT = V_HEAD_DIM + BF16_ROWS
VT_ROWS = N_HEADS * V_SLOT
HEAD_SLOT = LANES
ROPE_LO = QK_NOPE_DIM
ROPE_HALF = QK_ROPE_DIM // 2
H_WIDTH = POOL_WIDTH + Q_LORA_RANK + KV_LORA_RANK + LANES
POOL_HALO = 8
VMEM_LIMIT = 60 * 1024 * 1024

PROJ_TILE = 1024
ATTN_Q_TILE = 512
ATTN_KEY_CHUNK = 256
ATTN_HEADS_PER_STEP = 8
ATTN_SKEW = 3
FFN_TILE = 512
FFN_SUBTILE = 256
assert FFN_TILE % FFN_SUBTILE == 0
FFN_CHUNKS = ((0, 768), (768, 1536), (1536, 2304), (2304, FFN_HIDDEN))
assert all(lo % MXU_COLS == 0 for lo, _ in FFN_CHUNKS)

BF16 = jnp.bfloat16
F32 = jnp.float32


def _rope(t, cos, sin_signed):
    partner = pltpu.roll(t, LANES - ROPE_HALF, 1)
    return t * cos + partner * sin_signed


def _rms(t, g):
    y = t * lax.rsqrt(jnp.mean(t * t, axis=-1, keepdims=True) + RMS_EPS)
    return y * g


def _layer_norm(y, g, b):
    mu = jnp.mean(y, axis=-1, keepdims=True)
    d = y - mu
    var = jnp.mean(d * d, axis=-1, keepdims=True)
    return d * lax.rsqrt(var + LN_EPS) * g + b


def _rope_table_kernel(invf_ref, pos_ref, cos_ref, sin_ref):
    pos = pos_ref[...].astype(F32)
    for f in range(ROPE_HALF):
        ang = pos * invf_ref[f]
        cos_ref[:, f, :] = jnp.cos(ang)
        sin_ref[:, f, :] = jnp.sin(ang)


def _rope_lanes(t, outside):
    tm = t.shape[1]
    full = jnp.concatenate([jnp.full((ROPE_LO, tm), outside, F32), t, t,
                            jnp.zeros((LANES - ROPE_LO - QK_ROPE_DIM, tm), F32)], axis=0)
    return full.T


def _proj_kernel(x_ref, cos_ref, sin_ref, w_in_ref, qg_ref, wq_ref, kvg_ref, wk_ref, wv_ref, ones_ref,
                 u_ref, q_ref, k_ref, v_ref, *, q_scale):
    x = x_ref[0].astype(BF16)
    h = jnp.dot(x, w_in_ref[...], preferred_element_type=F32)
    u_ref[0] = h[:, :POOL_WIDTH]
    o_kv = POOL_WIDTH + Q_LORA_RANK
    o_kr = o_kv + KV_LORA_RANK
    cq = _rms(h[:, POOL_WIDTH:o_kv], qg_ref[...])
    ckv = _rms(h[:, o_kv:o_kr], kvg_ref[...])
    kr = h[:, o_kr:]

    groups = cos_ref.shape[0]
    cos_t = jnp.concatenate([cos_ref[j] for j in range(groups)], axis=1)
    sin_t = jnp.concatenate([sin_ref[j] for j in range(groups)], axis=1)
    cos = _rope_lanes(cos_t, 1.0)
    sin = _rope_lanes(sin_t, 0.0)
    first_half = lax.broadcasted_iota(jnp.int32, cos.shape, 1) < ROPE_LO + ROPE_HALF
    sin = jnp.where(first_half, -sin, sin)

    cq_b = cq.astype(BF16)
    ckv_b = ckv.astype(BF16)
    nt_dims = (((1,), (1,)), ((), ()))
    qt = lax.dot_general(wq_ref[...], cq_b, nt_dims, preferred_element_type=F32)
    k_nope = jnp.dot(ckv_b, wk_ref[...], preferred_element_type=F32)
    vt = lax.dot_general(wv_ref[...], ckv_b, nt_dims, preferred_element_type=F32)
    v_ref[0] = (vt + ones_ref[...]).astype(BF16)

    cos_q = cos_t * q_scale
    sin_q = sin_t * q_scale
    k_rope = _rope(kr, cos, sin)
    for hd in range(N_HEADS):
        sl = slice(hd * HEAD_SLOT, (hd + 1) * HEAD_SLOT)
        k_ref[0, :, sl] = (k_nope[:, sl] + k_rope).astype(BF16)
        base = hd * HEAD_SLOT
        r1 = qt[base + ROPE_LO:base + ROPE_LO + ROPE_HALF]
        r2 = qt[base + ROPE_LO + ROPE_HALF:base + ROPE_LO + QK_ROPE_DIM]
        q_ref[0, base:base + ROPE_LO, :] = (qt[base:base + ROPE_LO] * q_scale).astype(BF16)
        q_ref[0, base + ROPE_LO:base + HEAD_SLOT, :] = jnp.concatenate(
            [r1 * cos_q - r2 * sin_q, r2 * cos_q + r1 * sin_q,
             jnp.zeros((HEAD_SLOT - QK_HEAD_DIM, qt.shape[1]), F32)], axis=0).astype(BF16)


def _row_groups(t, op):
    return op(t.reshape(t.shape[0] // SUBLANES, SUBLANES, t.shape[1]), axis=0)


def _attn_kernel(q_ref, qn_ref, k_ref, kn_ref, vt_ref, zero_ref, w32_a, w32_b, w32_c,
                 o_ref, w16_a, w16_b, w16_c, s_ref, mx_ref, *, kc):
    seq = k_ref.shape[1]
    tq = q_ref.shape[2]
    hp = k_ref.shape[2] // HEAD_SLOT
    n_chunks = seq // kc
    assert hp % 2 == 0

    last_tile = pl.program_id(2) == pl.num_programs(2) - 1

    def score_chunk(q, h, c, next_tile=False):
        hs = slice(h * HEAD_SLOT, (h + 1) * HEAD_SLOT)
        ks = slice(c * kc, (c + 1) * kc)
        keys = k_ref[0, ks, hs]
        if next_tile:
            keys = jnp.where(last_tile, kn_ref[0, ks, :], keys)
        s = jnp.dot(keys, q[0, hs, :], preferred_element_type=F32)
        s_ref[h % 2, ks, :tq] = s
        return _row_groups(s, jnp.max)

    def fold(cms):
        mx = cms[0]
        for cm in cms[1:]:
            mx = jnp.maximum(mx, cm)
        return mx

    @pl.when((pl.program_id(0) == 0) & (pl.program_id(2) == 0))
    def _():
        mx_ref[...] = fold([score_chunk(q_ref, 0, c) for c in range(n_chunks)])

    mx = mx_ref[...]
    zero_bits = zero_ref[0:1, :]
    outs = []
    for h in range(hp):
        m = jnp.max(mx, axis=0, keepdims=True)
        ds = slice(h * V_SLOT, (h + 1) * V_SLOT)
        cms = []
        acc = None
        for c in range(n_chunks):
            ks = slice(c * kc, (c + 1) * kc)
            cms.append(score_chunk(q_ref, h + 1, c) if h + 1 < hp
                       else score_chunk(qn_ref, 0, c, next_tile=True))
            m_c = m
            if c >= ATTN_SKEW:
                dep = lax.bitcast_convert_type(cms[c - ATTN_SKEW][0:1, :], jnp.int32) & zero_bits
                m_c = m + lax.bitcast_convert_type(dep, F32)
            p = jnp.exp2(s_ref[h % 2, ks, :tq] - m_c)
            o = jnp.dot(vt_ref[0, ds, ks], p.astype(BF16), preferred_element_type=F32)
            acc = o if acc is None else acc + o
        outs.append(acc[:V_HEAD_DIM] / acc[V_HEAD_DIM:V_HEAD_DIM + 1])
        mx = fold(cms)
    mx_ref[...] = mx
    o_ref[0] = jnp.concatenate(outs, axis=0).T.astype(BF16)
    for w32, w16 in ((w32_a, w16_a), (w32_b, w16_b), (w32_c, w16_c)):
        w16[...] = w32[...].astype(BF16)


def _fill_halo_buffer(i, n_tiles, u_ref, uprev_ref, unext_ref, ext_ref):
    tm = u_ref.shape[0]
    ext_ref[0:POOL_HALO] = jnp.where(i > 0, uprev_ref[...], 0.0)
    ext_ref[POOL_HALO:POOL_HALO + tm] = u_ref[...]
    ext_ref[POOL_HALO + tm:] = jnp.where(i < n_tiles - 1, unext_ref[...], 0.0)


def _mix_stages(emit, seq_row0, r0, th, seq, alpha, attn_ref, x_ref,
                pw_ref, ps_ref, wo_ref, g_ref, b_ref, ext_ref):
    row = seq_row0 + r0 + lax.broadcasted_iota(jnp.int32, (th, 1), 0)
    mixed = []
    for g, w in enumerate(POOL_WINDOWS):
        cols = slice(g * POOL_GROUP_DIM, (g + 1) * POOL_GROUP_DIM)
        back = w // 2
        win = ext_ref[pl.ds(POOL_HALO + r0 - back, th), cols]
        for d in range(1 - back, w - back):
            win = win + ext_ref[pl.ds(POOL_HALO + r0 + d, th), cols]
        cnt = (jnp.minimum(row + (w - back), seq) - jnp.maximum(row - back, 0)).astype(F32)
        pg = win / cnt - ext_ref[pl.ds(POOL_HALO + r0, th), cols]
        og = jnp.dot(pg.astype(BF16), pw_ref[g], preferred_element_type=F32) * ps_ref[:, cols]
        mixed.append(og.astype(BF16))
        yield
    mixed.append(attn_ref[r0:r0 + th, :])
    acc = jnp.dot(jnp.concatenate(mixed, axis=1), wo_ref[...], preferred_element_type=F32)
    y = alpha * x_ref[r0:r0 + th, :] + acc
    emit(_layer_norm(y, g_ref[...], b_ref[...]))
    yield


def _ffn_stages(emit, x, alpha, wg_ref, wu_ref, wd_ref, g_ref, b_ref):
    xb = x.astype(BF16)
    ffn = None
    for lo, hi in FFN_CHUNKS:
        gate = jnp.dot(xb, wg_ref[:, lo:hi], preferred_element_type=F32)
        up = jnp.dot(xb, wu_ref[:, lo:hi], preferred_element_type=F32)
        hid = gate * (1.0 / (1.0 + jnp.exp(-gate))) * up
        part = jnp.dot(hid.astype(BF16), wd_ref[lo:hi, :], preferred_element_type=F32)
        ffn = part if ffn is None else ffn + part
        yield
    y = alpha * x + ffn
    emit(_layer_norm(y, g_ref[...], b_ref[...]))
    yield


def _mixffn_kernel(u_ref, uprev_ref, unext_ref, attn_ref, x_ref,
                   u0_ref, unext0_ref, attn0_ref, x0_ref,
                   pw_ref, ps_ref, wo_ref, g1_ref, b1_ref, wg_ref, wu_ref, wd_ref, g2_ref, b2_ref,
                   out_ref, ext_ref, x1_ref, *, seq, alpha):
    s = pl.program_id(0)
    n_steps = pl.num_programs(0)
    tm = u_ref.shape[0]
    n_tiles = seq // tm
    th = FFN_SUBTILE
    mix_w = (pw_ref, ps_ref, wo_ref, g1_ref, b1_ref, ext_ref)

    def store(ref, slot, r0):
        def emit(v):
            ref[slot, r0:r0 + th, :] = v
        return emit

    def store_out(r0):
        def emit(v):
            out_ref[r0:r0 + th, :] = v
        return emit

    @pl.when(s == 0)
    def _():
        _fill_halo_buffer(0, n_tiles, u0_ref, unext0_ref, unext0_ref, ext_ref)
        for r0 in range(0, tm, th):
            for _ in _mix_stages(store(x1_ref, 0, r0), 0, r0, th, seq, alpha, attn0_ref, x0_ref, *mix_w):
                pass

    nxt = jnp.minimum(s + 1, n_steps - 1)
    x_cur = [x1_ref[s % 2, r0:r0 + th, :] for r0 in range(0, tm, th)]
    _fill_halo_buffer(nxt % n_tiles, n_tiles, u_ref, uprev_ref, unext_ref, ext_ref)
    for j, r0 in enumerate(range(0, tm, th)):
        ffn = _ffn_stages(store_out(r0), x_cur[j], alpha, wg_ref, wu_ref, wd_ref, g2_ref, b2_ref)
        mix = _mix_stages(store(x1_ref, (s + 1) % 2, r0), (nxt % n_tiles) * tm, r0, th, seq, alpha,
                          attn_ref, x_ref, *mix_w)
        for _ in range(len(FFN_CHUNKS) + 1):
            next(ffn)
            next(mix)
        assert next(mix, None) is None and next(ffn, None) is None


def _full(shape):
    return pl.BlockSpec(shape, lambda *_: (0,) * len(shape))


def _params(sem):
    return pltpu.CompilerParams(dimension_semantics=sem, vmem_limit_bytes=VMEM_LIMIT)


def _pad_heads(w, dim):
    r = w.shape[0]
    w = w.reshape(r, N_HEADS, dim)
    return jnp.pad(w, ((0, 0), (0, 0), (0, HEAD_SLOT - dim))).reshape(r, N_HEADS * HEAD_SLOT)


def _rope_slot(w):
    return jnp.concatenate([w, w[..., :ROPE_HALF], jnp.zeros_like(w[..., :ROPE_HALF])], axis=-1)


def _layer(x, cos_t, sin_t, w_in, pool_w, pool_scale, q_norm_g, w_q_up, kv_norm_g, w_k_up, w_v_up,
           w_o, ln1_g, ln1_b, w_gate, w_up, w_down, ln2_g, ln2_b, alpha):
    bsz, seq, _ = x.shape
    tm, tq, tf, hp = PROJ_TILE, ATTN_Q_TILE, FFN_TILE, ATTN_HEADS_PER_STEP
    assert seq % tm == 0 and seq % tq == 0 and seq % tf == 0 and N_HEADS % hp == 0
    assert len(POOL_WINDOWS) == len(FFN_CHUNKS)
    o_kr = POOL_WIDTH + Q_LORA_RANK + KV_LORA_RANK

    w_in_ext = jnp.concatenate(
        [w_in[:, :o_kr], jnp.zeros((D_MODEL, ROPE_LO), w_in.dtype), _rope_slot(w_in[:, o_kr:])],
        axis=1).astype(BF16)
    wq = _pad_heads(w_q_up, QK_HEAD_DIM).T.astype(BF16)
    wk = _pad_heads(w_k_up, QK_NOPE_DIM).astype(BF16)
    wv = jnp.pad(w_v_up.T.reshape(N_HEADS, V_HEAD_DIM, KV_LORA_RANK),
                 ((0, 0), (0, V_SLOT - V_HEAD_DIM), (0, 0))).reshape(VT_ROWS, KV_LORA_RANK).astype(BF16)
    ones_col = (jnp.arange(VT_ROWS) % V_SLOT == V_HEAD_DIM).astype(F32)[:, None]
    q_scale = (QK_HEAD_DIM ** -0.5) * math.log2(math.e)

    tok = lambda w: pl.BlockSpec((1, tm, w), lambda b, i: (b, i, 0))
    rope = lambda: pl.BlockSpec((tm // LANES, ROPE_HALF, LANES),
                                lambda b, i: (b * (seq // tm) + i, 0, 0))
    u, q, k, v = pl.pallas_call(
        functools.partial(_proj_kernel, q_scale=q_scale),
        grid=(bsz, seq // tm),
        in_specs=[tok(D_MODEL), rope(), rope(), _full((D_MODEL, H_WIDTH)),
                  _full((1, Q_LORA_RANK)), _full((N_HEADS * HEAD_SLOT, Q_LORA_RANK)),
                  _full((1, KV_LORA_RANK)), _full((KV_LORA_RANK, N_HEADS * HEAD_SLOT)),
                  _full((VT_ROWS, KV_LORA_RANK)), _full((VT_ROWS, 1))],
        out_specs=[tok(POOL_WIDTH),
                   pl.BlockSpec((1, N_HEADS * HEAD_SLOT, tm), lambda b, i: (b, 0, i)),
                   tok(N_HEADS * HEAD_SLOT),
                   pl.BlockSpec((1, VT_ROWS, tm), lambda b, i: (b, 0, i))],
        out_shape=[jax.ShapeDtypeStruct((bsz, seq, POOL_WIDTH), F32),
                   jax.ShapeDtypeStruct((bsz, N_HEADS * HEAD_SLOT, seq), BF16),
                   jax.ShapeDtypeStruct((bsz, seq, N_HEADS * HEAD_SLOT), BF16),
                   jax.ShapeDtypeStruct((bsz, VT_ROWS, seq), BF16)],
        compiler_params=_params(("parallel", "parallel")),
        name="proj",
    )(x, cos_t, sin_t, w_in_ext, q_norm_g[None, :], wq, kv_norm_g[None, :], wk, wv, ones_col)

    n_groups, n_q = N_HEADS // hp, seq // tq
    assert n_groups == 1

    def next_tile(b, g, i):
        wraps = (i + 1) // n_q
        more = b + wraps < bsz
        return (jnp.minimum(b + wraps, bsz - 1), g, jnp.where(more, (i + 1) % n_q, n_q - 1))

    steps = bsz * n_groups * n_q
    w_rows, wd_rows = D_MODEL // steps, FFN_HIDDEN // (steps // 2)
    assert w_rows * steps == D_MODEL and wd_rows * (steps // 2) == FFN_HIDDEN
    assert w_rows % BF16_ROWS == 0 and wd_rows % BF16_ROWS == 0
    step_of = lambda b, g, i: (b * n_groups + g) * n_q + i
    w_slice = pl.BlockSpec((w_rows, FFN_HIDDEN), lambda b, g, i: (step_of(b, g, i), 0))
    wd_slice = pl.BlockSpec((wd_rows, D_MODEL), lambda b, g, i: (step_of(b, g, i) // 2, 0))
    attn, wg16, wu16, wd16 = pl.pallas_call(
        functools.partial(_attn_kernel, kc=ATTN_KEY_CHUNK),
        grid=(bsz, n_groups, n_q),
        in_specs=[pl.BlockSpec((1, hp * HEAD_SLOT, tq), lambda b, g, i: (b, g, i)),
                  pl.BlockSpec((1, hp * HEAD_SLOT, tq), next_tile),
                  pl.BlockSpec((1, seq, hp * HEAD_SLOT), lambda b, g, i: (b, 0, g)),
                  pl.BlockSpec((1, seq, HEAD_SLOT), lambda b, g, i: (jnp.minimum(b + 1, bsz - 1), 0, 0)),
                  pl.BlockSpec((1, hp * V_SLOT, seq), lambda b, g, i: (b, g, 0)),
                  _full((SUBLANES, tq)), w_slice, w_slice, wd_slice],
        out_specs=[pl.BlockSpec((1, tq, hp * V_HEAD_DIM), lambda b, g, i: (b, i, g)),
                   w_slice, w_slice, wd_slice],
        out_shape=[jax.ShapeDtypeStruct((bsz, seq, ATTN_WIDTH), BF16)]
                  + [jax.ShapeDtypeStruct(w.shape, BF16) for w in (w_gate, w_up, w_down)],
        scratch_shapes=[pltpu.VMEM((2, seq, tq + LANES), F32),
                        pltpu.VMEM((SUBLANES, tq), F32)],
        compiler_params=_params(("arbitrary", "arbitrary", "arbitrary")),
        name="attn",
    )(q, q, k, k, v, jnp.zeros((SUBLANES, tq), jnp.int32), w_gate, w_up, w_down)

    rows = bsz * seq
    n_steps = rows // tf
    halo_blocks = tf // POOL_HALO
    last_halo = rows // POOL_HALO - 1
    ahead = lambda s: jnp.minimum(s + 1, n_steps - 1)
    row_blk = lambda w, idx, **kw: pl.BlockSpec((tf, w), lambda s: (idx(s), 0), **kw)
    halo = lambda idx, **kw: pl.BlockSpec((POOL_HALO, POOL_WIDTH), lambda s: (idx(s), 0), **kw)
    first = lambda s: 0
    single = dict(pipeline_mode=pl.Buffered(1))
    once = lambda shape: pl.BlockSpec(shape, lambda s: (0,) * len(shape), **single)
    u2 = u.reshape(rows, POOL_WIDTH)
    attn2 = attn.reshape(rows, ATTN_WIDTH)
    xr = x.reshape(rows, D_MODEL)
    x2 = pl.pallas_call(
        functools.partial(_mixffn_kernel, seq=seq, alpha=alpha),
        grid=(n_steps,),
        in_specs=[row_blk(POOL_WIDTH, ahead),
                  halo(lambda s: jnp.maximum(ahead(s) * halo_blocks - 1, 0)),
                  halo(lambda s: jnp.minimum((ahead(s) + 1) * halo_blocks, last_halo)),
                  row_blk(ATTN_WIDTH, ahead), row_blk(D_MODEL, ahead),
                  row_blk(POOL_WIDTH, first, **single), halo(lambda s: halo_blocks, **single),
                  row_blk(ATTN_WIDTH, first, **single), row_blk(D_MODEL, first, **single),
                  once((len(POOL_WINDOWS), POOL_GROUP_DIM, POOL_GROUP_DIM)), once((1, POOL_WIDTH)),
                  once((D_MODEL, D_MODEL)), once((1, D_MODEL)), once((1, D_MODEL)),
                  once((D_MODEL, FFN_HIDDEN)), once((D_MODEL, FFN_HIDDEN)), once((FFN_HIDDEN, D_MODEL)),
                  once((1, D_MODEL)), once((1, D_MODEL))],
        out_specs=pl.BlockSpec((tf, D_MODEL), lambda s: (s, 0)),
        out_shape=jax.ShapeDtypeStruct((rows, D_MODEL), F32),
        scratch_shapes=[pltpu.VMEM((tf + 2 * POOL_HALO, POOL_WIDTH), F32),
                        pltpu.VMEM((2, tf, D_MODEL), F32)],
        compiler_params=_params(("arbitrary",)),
        name="mixffn",
    )(u2, u2, u2, attn2, xr, u2, u2, attn2, xr,
      pool_w.astype(BF16), pool_scale[None, :], w_o.astype(BF16), ln1_g[None, :], ln1_b[None, :],
      wg16, wu16, wd16, ln2_g[None, :], ln2_b[None, :])
    return x2.reshape(bsz, seq, D_MODEL)


def kernel(x, positions, w_in, pool_w, pool_scale, q_norm_g, w_q_up, kv_norm_g, w_k_up, w_v_up, w_o, ln1_g, ln1_b, w_gate, w_up, w_down, ln2_g, ln2_b):
    depth = w_in.shape[0]
    alpha = (2.0 * depth) ** 0.25
    inv_freq = 1.0 / (ROPE_THETA ** (jnp.arange(0, QK_ROPE_DIM, 2, dtype=F32) / QK_ROPE_DIM))
    tokens = positions.size
    groups = tokens // LANES
    table = (groups, ROPE_HALF, LANES)
    cos_t, sin_t = pl.pallas_call(
        _rope_table_kernel,
        in_specs=[pl.BlockSpec(memory_space=pltpu.SMEM), _full((groups, LANES))],
        out_specs=[_full(table), _full(table)],
        out_shape=[jax.ShapeDtypeStruct(table, F32)] * 2,
        grid=(1,),
        name="rope_table",
    )(inv_freq, positions.reshape(groups, LANES))
    for l in range(depth):
        x = _layer(x, cos_t, sin_t, w_in[l], pool_w[l], pool_scale[l], q_norm_g[l], w_q_up[l],
                   kv_norm_g[l], w_k_up[l], w_v_up[l], w_o[l], ln1_g[l], ln1_b[l],
                   w_gate[l], w_up[l], w_down[l], ln2_g[l], ln2_b[l], alpha)
    return x
```
